```python
import math
import jax, jax.numpy as jnp
from jax import lax
import numpy as np

D_MODEL = 1024
BATCH = 8
SEQ = 4096
DEPTH = 4
DEC_BATCH = 8
DEC_SEQ = 64
PAST_LEN = 4096

CHUNK = 64
N_HEADS = 16
N_KV_HEADS = 2
HEAD_DIM = 64
GQA_GROUP = N_HEADS // N_KV_HEADS
QKV_DIM = (N_HEADS + 2 * N_KV_HEADS) * HEAD_DIM
WINDOW = 128
WIN_CHUNKS = WINDOW // CHUNK
ROPE_THETA = 10000.0
SSM_GROUP = 16
SSM_GROUPS = D_MODEL // SSM_GROUP
SSM_STATE = 64
D_FF = -(-8 * D_MODEL // (3 * 256)) * 256
N_ATTN_LAYERS = (DEPTH + 1) // 2
N_SSM_LAYERS = DEPTH // 2
DN_ALPHA = (2.0 * DEPTH) ** 0.25
DN_BETA = (8.0 * DEPTH) ** -0.25
LN_EPS = 1e-5
NEG_INF = -1e30

kernel_name = "hybrid_swa_s5_streaming_step"


def layer_norm(x, g, b):
    xf = x.astype(jnp.float32)
    mu = xf.mean(-1, keepdims=True)
    var = jnp.square(xf - mu).mean(-1, keepdims=True)
    y = (xf - mu) * lax.rsqrt(var + LN_EPS) * g.astype(jnp.float32) + b.astype(jnp.float32)
    return y.astype(x.dtype)


def rope(x, pos):
    half = HEAD_DIM // 2
    inv = ROPE_THETA ** (-jnp.arange(half, dtype=jnp.float32) / half)
    ang = pos.astype(jnp.float32)[:, None] * inv[None, :]
    cos = jnp.cos(ang)[None, :, None, :]
    sin = jnp.sin(ang)[None, :, None, :]
    xf = x.astype(jnp.float32)
    x1, x2 = xf[..., :half], xf[..., half:]
    return jnp.concatenate([x1 * cos - x2 * sin, x2 * cos + x1 * sin], axis=-1).astype(x.dtype)


def sink_softmax(s, sink):
    m = jnp.maximum(s.max(-1, keepdims=True), sink)
    e = jnp.exp(s - m)
    return e / (e.sum(-1, keepdims=True) + jnp.exp(sink - m))


def qkv_proj(x, w_qkv, b_qkv, pos):
    B, T, _ = x.shape
    qkv = x @ w_qkv + b_qkv
    nq = N_HEADS * HEAD_DIM
    nk = N_KV_HEADS * HEAD_DIM
    q = rope(qkv[..., :nq].reshape(B, T, N_HEADS, HEAD_DIM), pos)
    k = rope(qkv[..., nq:nq + nk].reshape(B, T, N_KV_HEADS, HEAD_DIM), pos)
    v = qkv[..., nq + nk:].reshape(B, T, N_KV_HEADS, HEAD_DIM)
    return q.reshape(B, T, N_KV_HEADS, GQA_GROUP, HEAD_DIM), k, v


def attn_prompt(x, w_qkv, b_qkv, sinks, w_o, b_o):
    B, S, _ = x.shape
    NC = S // CHUNK
    q, k, v = qkv_proj(x, w_qkv, b_qkv, jnp.arange(S))
    pad = ((0, 0), (WIN_CHUNKS * CHUNK, 0), (0, 0), (0, 0))
    kc = jnp.pad(k, pad).reshape(B, NC + WIN_CHUNKS, CHUNK, N_KV_HEADS, HEAD_DIM)
    vc = jnp.pad(v, pad).reshape(B, NC + WIN_CHUNKS, CHUNK, N_KV_HEADS, HEAD_DIM)
    kb = jnp.concatenate([kc[:, j:j + NC] for j in range(WIN_CHUNKS + 1)], axis=2)
    vb = jnp.concatenate([vc[:, j:j + NC] for j in range(WIN_CHUNKS + 1)], axis=2)
    key_pos = (jnp.arange(NC)[:, None] - WIN_CHUNKS) * CHUNK + jnp.arange((WIN_CHUNKS + 1) * CHUNK)[None, :]
    qb = q.reshape(B, NC, CHUNK, N_KV_HEADS, GQA_GROUP, HEAD_DIM)
    s = jnp.einsum('bnqkgd,bnpkd->bnkgqp', qb, kb).astype(jnp.float32) * (HEAD_DIM ** -0.5)
    s = jnp.where((key_pos >= 0)[None, :, None, None, None, :], s, NEG_INF)
    p = sink_softmax(s, sinks.astype(jnp.float32).reshape(1, 1, N_KV_HEADS, GQA_GROUP, 1, 1))
    o = jnp.einsum('bnkgqp,bnpkd->bnqkgd', p.astype(vb.dtype), vb).reshape(B, S, N_HEADS * HEAD_DIM)
    y = o @ w_o + b_o
    return y, k[:, S - WINDOW:], v[:, S - WINDOW:]


def attn_sample(x, ck, cv, w_qkv, b_qkv, sinks, w_o, b_o):
    B, T, _ = x.shape
    R = ck.shape[1]
    q, k, v = qkv_proj(x, w_qkv, b_qkv, PAST_LEN + jnp.arange(T))
    kk = jnp.concatenate([ck.astype(k.dtype), k], axis=1)
    vv = jnp.concatenate([cv.astype(v.dtype), v], axis=1)
    s = jnp.einsum('btkgd,bpkd->bkgtp', q, kk).astype(jnp.float32) * (HEAD_DIM ** -0.5)
    p = sink_softmax(s, sinks.astype(jnp.float32).reshape(1, N_KV_HEADS, GQA_GROUP, 1, 1))
    o = jnp.einsum('bkgtp,bpkd->btkgd', p.astype(vv.dtype), vv).reshape(B, T, N_HEADS * HEAD_DIM)
    y = o @ w_o + b_o
    return y, kk[:, -R:], vv[:, -R:]


def s5_discretize(log_dt, a_re, a_im, b_re, b_im, c_re, c_im):
    dt = jnp.exp(log_dt.astype(jnp.float32))[:, None]
    a = lax.complex(a_re.astype(jnp.float32), a_im.astype(jnp.float32))
    a_bar = jnp.exp(dt * a)
    bmat = lax.complex(b_re.astype(jnp.float32), b_im.astype(jnp.float32))
    b_bar = ((a_bar - 1.0) / a)[..., None] * bmat
    c = lax.complex(c_re.astype(jnp.float32), c_im.astype(jnp.float32))
    return a_bar, b_bar, c


def _lin_combine(e1, e2):
    a1, b1 = e1
    a2, b2 = e2
    return a1 * a2, a2 * b1 + b2


def s5_block_scan(u, s0, a_bar, b_bar, c):
    bu = jnp.einsum('gpc,blgc->blgp', b_bar, u.astype(jnp.complex64))
    bu = bu.at[:, 0].add(a_bar * s0)
    a = jnp.broadcast_to(a_bar, bu.shape)
    _, s = lax.associative_scan(_lin_combine, (a, bu), axis=1)
    y = jnp.einsum('gcp,blgp->blgc', c, s).real
    return y, s[:, -1]


def s5_glu(x, u, y, d, w_glu, b_glu):
    B, T, _ = x.shape
    y = y.reshape(B, T, D_MODEL) + d.astype(jnp.float32) * u
    z = jax.nn.gelu(y)
    gv = z @ w_glu.astype(jnp.float32) + b_glu.astype(jnp.float32)
    return (gv[..., :D_MODEL] * jax.nn.sigmoid(gv[..., D_MODEL:])).astype(x.dtype)


def ssm_prompt(x, w_in, b_in, log_dt, a_re, a_im, b_re, b_im, c_re, c_im, d, w_glu, b_glu):
    B, S, _ = x.shape
    NC = S // CHUNK
    a_bar, b_bar, c = s5_discretize(log_dt, a_re, a_im, b_re, b_im, c_re, c_im)
    u = (x @ w_in + b_in).astype(jnp.float32)
    uc = u.reshape(B, NC, CHUNK, SSM_GROUPS, SSM_GROUP).transpose(1, 0, 2, 3, 4)
    s0 = jnp.zeros((B, SSM_GROUPS, SSM_STATE), jnp.complex64)

    def step(s, u_chunk):
        y_chunk, s_new = s5_block_scan(u_chunk, s, a_bar, b_bar, c)
        return s_new, y_chunk

    s_last, ys = lax.scan(step, s0, uc)
    y = ys.transpose(1, 0, 2, 3, 4)
    return s5_glu(x, u, y, d, w_glu, b_glu), s_last.real, s_last.imag


def ssm_sample(x, st_re, st_im, w_in, b_in, log_dt, a_re, a_im, b_re, b_im, c_re, c_im, d, w_glu, b_glu):
    B, T, _ = x.shape
    a_bar, b_bar, c = s5_discretize(log_dt, a_re, a_im, b_re, b_im, c_re, c_im)
    u = (x @ w_in + b_in).astype(jnp.float32)
    s0 = lax.complex(st_re.astype(jnp.float32), st_im.astype(jnp.float32))
    y, s_last = s5_block_scan(u.reshape(B, T, SSM_GROUPS, SSM_GROUP), s0, a_bar, b_bar, c)
    return s5_glu(x, u, y, d, w_glu, b_glu), s_last.real, s_last.imag


def swiglu(x, w_up, w_down):
    h = x @ w_up
    return (jax.nn.silu(h[..., :D_FF]) * h[..., D_FF:]) @ w_down


def setup_inputs(seed: int = 0) -> dict:
    key = jax.random.key(seed)
    ks = jax.random.split(key, 32)
    f32 = jnp.float32

    def nrm(k, shape, scale):
        return jax.random.normal(k, shape, f32) * scale

    cache_rows = min(WINDOW, PAST_LEN)
    n_idx = jnp.arange(SSM_STATE, dtype=f32)
    return {
        "x_prompt": nrm(ks[0], (BATCH, SEQ, D_MODEL), 1.0),
        "x_sample": nrm(ks[1], (DEC_BATCH, DEC_SEQ, D_MODEL), 1.0),
        "cache_k": nrm(ks[2], (N_ATTN_LAYERS, DEC_BATCH, cache_rows, N_KV_HEADS, HEAD_DIM), 1.0),
        "cache_v": nrm(ks[3], (N_ATTN_LAYERS, DEC_BATCH, cache_rows, N_KV_HEADS, HEAD_DIM), 1.0),
        "state_ssm_re": nrm(ks[4], (N_SSM_LAYERS, DEC_BATCH, SSM_GROUPS, SSM_STATE), 0.1),
        "state_ssm_im": nrm(ks[5], (N_SSM_LAYERS, DEC_BATCH, SSM_GROUPS, SSM_STATE), 0.1),
        "attn_w_qkv": nrm(ks[6], (N_ATTN_LAYERS, D_MODEL, QKV_DIM), D_MODEL ** -0.5),
        "attn_b_qkv": nrm(ks[7], (N_ATTN_LAYERS, QKV_DIM), 0.02),
        "attn_sinks": nrm(ks[8], (N_ATTN_LAYERS, N_HEADS), 0.5),
        "attn_w_o": nrm(ks[9], (N_ATTN_LAYERS, N_HEADS * HEAD_DIM, D_MODEL), (N_HEADS * HEAD_DIM) ** -0.5 * DN_BETA),
        "attn_b_o": nrm(ks[10], (N_ATTN_LAYERS, D_MODEL), 0.02),
        "ssm_w_in": nrm(ks[11], (N_SSM_LAYERS, D_MODEL, D_MODEL), D_MODEL ** -0.5),
        "ssm_b_in": nrm(ks[12], (N_SSM_LAYERS, D_MODEL), 0.02),
        "ssm_log_dt": jax.random.uniform(ks[13], (N_SSM_LAYERS, SSM_GROUPS), f32, math.log(1e-3), math.log(1e-1)),
        "ssm_a_re": -0.5 + nrm(ks[14], (N_SSM_LAYERS, SSM_GROUPS, SSM_STATE), 0.01),
        "ssm_a_im": jnp.pi * n_idx + nrm(ks[15], (N_SSM_LAYERS, SSM_GROUPS, SSM_STATE), 0.01),
        "ssm_b_re": nrm(ks[16], (N_SSM_LAYERS, SSM_GROUPS, SSM_STATE, SSM_GROUP), (2.0 * SSM_GROUP) ** -0.5),
        "ssm_b_im": nrm(ks[17], (N_SSM_LAYERS, SSM_GROUPS, SSM_STATE, SSM_GROUP), (2.0 * SSM_GROUP) ** -0.5),
        "ssm_c_re": nrm(ks[18], (N_SSM_LAYERS, SSM_GROUPS, SSM_GROUP, SSM_STATE), 0.5),
        "ssm_c_im": nrm(ks[19], (N_SSM_LAYERS, SSM_GROUPS, SSM_GROUP, SSM_STATE), 0.5),
        "ssm_d": nrm(ks[20], (N_SSM_LAYERS, D_MODEL), 1.0),
        "ssm_w_glu": jnp.concatenate([
            nrm(ks[21], (N_SSM_LAYERS, D_MODEL, D_MODEL), D_MODEL ** -0.5 * DN_BETA),
            nrm(ks[22], (N_SSM_LAYERS, D_MODEL, D_MODEL), D_MODEL ** -0.5)], axis=-1),
        "ssm_b_glu": nrm(ks[23], (N_SSM_LAYERS, 2 * D_MODEL), 0.02),
        "ffn_w_up": nrm(ks[24], (DEPTH, D_MODEL, 2 * D_FF), D_MODEL ** -0.5),
        "ffn_w_down": nrm(ks[25], (DEPTH, D_FF, D_MODEL), D_FF ** -0.5 * DN_BETA),
        "ln_gain": 1.0 + nrm(ks[26], (DEPTH, 2, D_MODEL), 0.02),
        "ln_bias": nrm(ks[27], (DEPTH, 2, D_MODEL), 0.02),
    }


def reference(x_prompt, x_sample, cache_k, cache_v, state_ssm_re, state_ssm_im,
              attn_w_qkv, attn_b_qkv, attn_sinks, attn_w_o, attn_b_o,
              ssm_w_in, ssm_b_in, ssm_log_dt, ssm_a_re, ssm_a_im, ssm_b_re, ssm_b_im,
              ssm_c_re, ssm_c_im, ssm_d, ssm_w_glu, ssm_b_glu,
              ffn_w_up, ffn_w_down, ln_gain, ln_bias):
    xp, xs = x_prompt, x_sample
    pk, pv, pre, pim = [], [], [], []
    sk, sv, sre, sim = [], [], [], []
    for i in range(DEPTH):
        l = i // 2
        if i % 2 == 0:
            mp, k_p, v_p = attn_prompt(xp, attn_w_qkv[l], attn_b_qkv[l], attn_sinks[l], attn_w_o[l], attn_b_o[l])
            ms, k_s, v_s = attn_sample(xs, cache_k[l], cache_v[l], attn_w_qkv[l], attn_b_qkv[l],
                                       attn_sinks[l], attn_w_o[l], attn_b_o[l])
            pk.append(k_p); pv.append(v_p); sk.append(k_s); sv.append(v_s)
        else:
            ssm_args = (ssm_w_in[l], ssm_b_in[l], ssm_log_dt[l], ssm_a_re[l], ssm_a_im[l],
                        ssm_b_re[l], ssm_b_im[l], ssm_c_re[l], ssm_c_im[l], ssm_d[l],
                        ssm_w_glu[l], ssm_b_glu[l])
            mp, r_p, i_p = ssm_prompt(xp, *ssm_args)
            ms, r_s, i_s = ssm_sample(xs, state_ssm_re[l], state_ssm_im[l], *ssm_args)
            pre.append(r_p); pim.append(i_p); sre.append(r_s); sim.append(i_s)
        xp = layer_norm(DN_ALPHA * xp + mp, ln_gain[i, 0], ln_bias[i, 0])
        xs = layer_norm(DN_ALPHA * xs + ms, ln_gain[i, 0], ln_bias[i, 0])
        xp = layer_norm(DN_ALPHA * xp + swiglu(xp, ffn_w_up[i], ffn_w_down[i]), ln_gain[i, 1], ln_bias[i, 1])
        xs = layer_norm(DN_ALPHA * xs + swiglu(xs, ffn_w_up[i], ffn_w_down[i]), ln_gain[i, 1], ln_bias[i, 1])
    return (xp, xs,
            jnp.stack(pk), jnp.stack(pv), jnp.stack(pre), jnp.stack(pim),
            jnp.stack(sk), jnp.stack(sv), jnp.stack(sre), jnp.stack(sim))
```

```python
import functools

import jax
import jax.numpy as jnp
from jax import lax
from jax.experimental import pallas as pl
from jax.experimental.pallas import tpu as pltpu

F32 = jnp.float32
BF16 = jnp.bfloat16

D_MODEL = 1024
DEPTH = 4
CHUNK = 64
N_HEADS = 16
N_KV_HEADS = 2
HEAD_DIM = 64
GQA_GROUP = N_HEADS // N_KV_HEADS
Q_DIM = N_HEADS * HEAD_DIM
KV_DIM = N_KV_HEADS * HEAD_DIM
QKV_DIM = Q_DIM + 2 * KV_DIM
WINDOW = 128
PAST_LEN = 4096
ROPE_THETA = 10000.0
SSM_GROUP = 16
SSM_GROUPS = D_MODEL // SSM_GROUP
SSM_STATE = 64
D_FF = 2816
DN_ALPHA = (2.0 * DEPTH) ** 0.25
LN_EPS = 1e-5
NEG_INF = -1e30

LANES = 128
FF_CHUNK = 256
N_FF_CHUNKS = D_FF // FF_CHUNK
SSM_BLOCK = CHUNK
SSM_BLOCK_DIM = SSM_GROUP * SSM_BLOCK
VMEM_LIMIT = 56 * 2 ** 20


def _params(n_axes):
    return pltpu.CompilerParams(dimension_semantics=("arbitrary",) * n_axes, vmem_limit_bytes=VMEM_LIMIT)


def _const_spec(shape):
    zeros = (0,) * len(shape)
    return pl.BlockSpec(shape, lambda *_: zeros, pipeline_mode=pl.Buffered(1))


def _layer_norm(y, gain, bias):
    mu = jnp.mean(y, axis=-1, keepdims=True)
    d = y - mu
    var = jnp.mean(d * d, axis=-1, keepdims=True)
    return d * lax.rsqrt(var + LN_EPS) * gain + bias


def _ffn_body(x_ref, wup_ref, wdn_ref, g_ref, b_ref, o_ref):
    x = x_ref[...]
    xb = x.astype(BF16)
    acc = DN_ALPHA * x
    for c in range(N_FF_CHUNKS):
        h = jnp.dot(xb, wup_ref[:, 2 * FF_CHUNK * c:2 * FF_CHUNK * (c + 1)], preferred_element_type=F32)
        gate = h[:, :FF_CHUNK]
        act = gate * jax.nn.sigmoid(gate) * h[:, FF_CHUNK:]
        acc = acc + jnp.dot(act.astype(BF16), wdn_ref[FF_CHUNK * c:FF_CHUNK * (c + 1), :],
                            preferred_element_type=F32)
    o_ref[...] = _layer_norm(acc, g_ref[...], b_ref[...])


def _ffn_ln(x, wup, wdn, gain, bias, tm):
    n = x.shape[0]
    return pl.pallas_call(
        _ffn_body,
        grid=(n // tm,),
        in_specs=[pl.BlockSpec((tm, D_MODEL), lambda i: (i, 0)),
                  _const_spec((D_MODEL, 2 * D_FF)), _const_spec((D_FF, D_MODEL)),
                  _const_spec((1, D_MODEL)), _const_spec((1, D_MODEL))],
        out_specs=pl.BlockSpec((tm, D_MODEL), lambda i: (i, 0)),
        out_shape=jax.ShapeDtypeStruct((n, D_MODEL), F32),
        compiler_params=_params(1),
        name="ffn_ln",
    )(x, wup, wdn, gain, bias)


def _qkv_body(x_ref, w_ref, b_ref, cos_ref, sa_ref, sb_ref, q_ref, k_ref, v_ref):
    xb = x_ref[...].astype(BF16)
    qkv = jnp.dot(xb, w_ref[...], preferred_element_type=F32) + b_ref[...]
    cos, sin_a, sin_b = cos_ref[...], sa_ref[...], sb_ref[...]
    for s in range((Q_DIM + KV_DIM) // LANES):
        t = qkv[:, s * LANES:(s + 1) * LANES]
        r = t * cos + pltpu.roll(t, LANES - HEAD_DIM // 2, 1) * sin_a + pltpu.roll(t, HEAD_DIM // 2, 1) * sin_b
        if s < Q_DIM // LANES:
            q_ref[:, s * LANES:(s + 1) * LANES] = (r * HEAD_DIM ** -0.5).astype(BF16)
        else:
            k_ref[...] = r
    v_ref[...] = qkv[:, Q_DIM + KV_DIM:]


def _qkv_rope(x, w, b, tabs, tm, tab_tiles):
    n = x.shape[0]
    tab_spec = pl.BlockSpec((tm, LANES), lambda i: (i % tab_tiles, 0))
    return pl.pallas_call(
        _qkv_body,
        grid=(n // tm,),
        in_specs=[pl.BlockSpec((tm, D_MODEL), lambda i: (i, 0)),
                  _const_spec((D_MODEL, QKV_DIM)), _const_spec((1, QKV_DIM)),
                  tab_spec, tab_spec, tab_spec],
        out_specs=[pl.BlockSpec((tm, Q_DIM), lambda i: (i, 0)),
                   pl.BlockSpec((tm, KV_DIM), lambda i: (i, 0)),
                   pl.BlockSpec((tm, KV_DIM), lambda i: (i, 0))],
        out_shape=[jax.ShapeDtypeStruct((n, Q_DIM), BF16),
                   jax.ShapeDtypeStruct((n, KV_DIM), F32),
                   jax.ShapeDtypeStruct((n, KV_DIM), F32)],
        compiler_params=_params(1),
        name="qkv_rope",
    )(x, w, b, *tabs)


def _rope_tables(pos):
    half = HEAD_DIM // 2
    lane = jnp.arange(LANES)
    inv = ROPE_THETA ** (-(lane % half).astype(F32) / half)
    ang = pos.astype(F32)[:, None] * inv[None, :]
    first = (lane % HEAD_DIM) < half
    sin = jnp.sin(ang)
    return jnp.cos(ang), jnp.where(first, -sin, 0.0), jnp.where(first, 0.0, sin)


def _attn_body(x_ref, q_ref, kp_ref, kc_ref, vp_ref, vc_ref, sink_ref, wo_ref, bo_ref, g_ref, b_ref,
               out_ref, o_scr, *, tq, masked):
    band = WINDOW + CHUNK
    kcat = jnp.concatenate([kp_ref[...], kc_ref[...]], axis=0)
    vcat = jnp.concatenate([vp_ref[...], vc_ref[...]], axis=0)
    lo_kv = lax.broadcasted_iota(jnp.int32, kcat.shape, 1) < HEAD_DIM
    k_sw = pltpu.roll(kcat, HEAD_DIM, 1)
    v_sw = pltpu.roll(vcat, HEAD_DIM, 1)
    k_dup = [jnp.where(lo_kv, kcat, k_sw).astype(BF16), jnp.where(lo_kv, k_sw, kcat).astype(BF16)]
    v_dup = [jnp.where(lo_kv, vcat, v_sw).astype(BF16), jnp.where(lo_kv, v_sw, vcat).astype(BF16)]
    lo_q = lax.broadcasted_iota(jnp.int32, (CHUNK, LANES), 1) < HEAD_DIM
    slabs_per_kv = GQA_GROUP * HEAD_DIM // LANES
    for j in range(tq // CHUNK):
        rows = slice(j * CHUNK, (j + 1) * CHUNK)
        for m in range(N_KV_HEADS):
            parts = []
            for jj in range(slabs_per_kv):
                col = (slabs_per_kv * m + jj) * LANES
                qs = q_ref[rows, col:col + LANES]
                parts.append(jnp.where(lo_q, qs, jnp.zeros_like(qs)))
                parts.append(jnp.where(lo_q, jnp.zeros_like(qs), qs))
            q_st = jnp.concatenate(parts, axis=0)
            kk = k_dup[m][j * CHUNK:j * CHUNK + band]
            vv = v_dup[m][j * CHUNK:j * CHUNK + band]
            s = lax.dot_general(q_st, kk, (((1,), (1,)), ((), ())), preferred_element_type=F32)
            if masked:
                key_pos = (pl.program_id(1) * tq + j * CHUNK - WINDOW
                           + lax.broadcasted_iota(jnp.int32, (1, band), 1))
                s = jnp.where(key_pos >= 0, s, NEG_INF)
            sink = sink_ref[m]
            mx = jnp.maximum(jnp.max(s, axis=-1, keepdims=True), sink)
            e = jnp.exp(s - mx)
            den = jnp.sum(e, axis=-1, keepdims=True) + jnp.exp(sink - mx)
            p = (e * (1.0 / den)).astype(BF16)
            o_st = jnp.dot(p, vv, preferred_element_type=F32)
            for jj in range(slabs_per_kv):
                first = o_st[(2 * jj) * CHUNK:(2 * jj + 1) * CHUNK]
                second = o_st[(2 * jj + 1) * CHUNK:(2 * jj + 2) * CHUNK]
                col = (slabs_per_kv * m + jj) * LANES
                o_scr[rows, col:col + LANES] = jnp.where(lo_q, first, second).astype(BF16)
    y = DN_ALPHA * x_ref[...] + jnp.dot(o_scr[...], wo_ref[...], preferred_element_type=F32) + bo_ref[...]
    out_ref[...] = _layer_norm(y, g_ref[...], b_ref[...])


def _attn_proj_ln(x, q, k_prev, k_cur, v_prev, v_cur, sink_cols, wo, bo, gain, bias, *, n_batch, tq, prev_map, masked):
    n = x.shape[0]
    tiles = n // (n_batch * tq)
    cur = lambda b, i: (b * tiles + i, 0)
    return pl.pallas_call(
        functools.partial(_attn_body, tq=tq, masked=masked),
        grid=(n_batch, tiles),
        in_specs=[pl.BlockSpec((tq, D_MODEL), cur), pl.BlockSpec((tq, Q_DIM), cur),
                  pl.BlockSpec((WINDOW, KV_DIM), prev_map), pl.BlockSpec((tq, KV_DIM), cur),
                  pl.BlockSpec((WINDOW, KV_DIM), prev_map), pl.BlockSpec((tq, KV_DIM), cur),
                  _const_spec((N_KV_HEADS, GQA_GROUP * CHUNK, 1)),
                  _const_spec((Q_DIM, D_MODEL)), _const_spec((1, D_MODEL)),
                  _const_spec((1, D_MODEL)), _const_spec((1, D_MODEL))],
        out_specs=pl.BlockSpec((tq, D_MODEL), cur),
        out_shape=jax.ShapeDtypeStruct((n, D_MODEL), F32),
        scratch_shapes=[pltpu.VMEM((tq, Q_DIM), BF16)],
        compiler_params=_params(2),
        name="attn_proj_ln",
    )(x, q, k_prev, k_cur, v_prev, v_cur, sink_cols, wo, bo, gain, bias)


def _ssm_prep_body(pcol_ref, prow_ref, bcol_ref, brow_ref, ccol_ref, dcol_ref,
                   t_ref, wtre_ref, wtim_ref, vre_ref, vim_ref, are_ref, aim_ref):
    def lam_pow(tau, a_re, a_im, dt):
        mag = jnp.exp(tau * (dt * a_re))
        ang = tau * (dt * a_im)
        return mag * jnp.cos(ang), mag * jnp.sin(ang)

    def zoh_factor(a_re, a_im, dt):
        l_re, l_im = lam_pow(1.0, a_re, a_im, dt)
        den = a_re * a_re + a_im * a_im
        n_re = l_re - 1.0
        return (n_re * a_re + l_im * a_im) / den, (l_im * a_re - n_re * a_im) / den

    pc = pcol_ref[0]
    ar_c, ai_c, dt_c = pc[:, 0:1], pc[:, 1:2], jnp.exp(pc[:, 2:3])
    fr_c, fi_c = zoh_factor(ar_c, ai_c, dt_c)
    bc = bcol_ref[0]
    bbr_c = fr_c * bc[:, :SSM_GROUP] - fi_c * bc[:, SSM_GROUP:]
    bbi_c = fr_c * bc[:, SSM_GROUP:] + fi_c * bc[:, :SSM_GROUP]
    pr = prow_ref[0]
    ar_r, ai_r, dt_r = pr[0:1], pr[1:2], jnp.exp(pr[2:3])
    fr_r, fi_r = zoh_factor(ar_r, ai_r, dt_r)
    br = brow_ref[0]
    bbr_r = fr_r * br[:SSM_GROUP] - fi_r * br[SSM_GROUP:]
    bbi_r = fr_r * br[SSM_GROUP:] + fi_r * br[:SSM_GROUP]
    cc = ccol_ref[0]

    lane = lax.broadcasted_iota(jnp.int32, (SSM_STATE, LANES), 1)
    lo = lane < SSM_BLOCK
    tau = (lane & (SSM_BLOCK - 1)).astype(F32)
    l0r, l0i = lam_pow(tau, ar_c, ai_c, dt_c)
    l1r, l1i = lam_pow(tau + 1.0, ar_c, ai_c, dt_c)
    lrr, lri = lam_pow(SSM_BLOCK - 1.0 - tau, ar_c, ai_c, dt_c)

    g_re, g_im = [], []
    for j in range(SSM_GROUP // 2):
        sl = slice(j * LANES, (j + 1) * LANES)
        cr = jnp.where(lo, cc[:, 2 * j:2 * j + 1], cc[:, 2 * j + 1:2 * j + 2])
        ci = jnp.where(lo, cc[:, SSM_GROUP + 2 * j:SSM_GROUP + 2 * j + 1],
                       cc[:, SSM_GROUP + 2 * j + 1:SSM_GROUP + 2 * j + 2])
        g_re.append(cr * l0r - ci * l0i)
        g_im.append(-(cr * l0i + ci * l0r))
        vre_ref[0, :, sl] = (cr * l1r - ci * l1i).astype(BF16)
        vim_ref[0, :, sl] = (-(cr * l1i + ci * l1r)).astype(BF16)
        b_r = jnp.where(lo, bbr_c[:, 2 * j:2 * j + 1], bbr_c[:, 2 * j + 1:2 * j + 2])
        b_i = jnp.where(lo, bbi_c[:, 2 * j:2 * j + 1], bbi_c[:, 2 * j + 1:2 * j + 2])
        wtre_ref[0, :, sl] = (b_r * lrr - b_i * lri).astype(BF16)
        wtim_ref[0, :, sl] = (b_r * lri + b_i * lrr).astype(BF16)
    g_re = jnp.concatenate(g_re, axis=1)
    g_im = jnp.concatenate(g_im, axis=1)
    kvec = (jnp.dot(bbr_r, g_re, preferred_element_type=F32, precision=lax.Precision.HIGHEST)
            + jnp.dot(bbi_r, g_im, preferred_element_type=F32, precision=lax.Precision.HIGHEST))
    klane = lax.broadcasted_iota(jnp.int32, kvec.shape, 1)
    krow = lax.broadcasted_iota(jnp.int32, kvec.shape, 0)
    kvec = kvec + jnp.where(klane == krow * SSM_BLOCK, dcol_ref[0], 0.0)

    t_in = lax.broadcasted_iota(jnp.int32, (SSM_BLOCK, LANES), 0)
    causal = (lax.broadcasted_iota(jnp.int32, (SSM_BLOCK, LANES), 1) & (SSM_BLOCK - 1)) >= t_in
    for c in range(SSM_GROUP):
        for j in range(SSM_GROUP // 2):
            src = jnp.broadcast_to(kvec[c:c + 1, j * LANES:(j + 1) * LANES], (SSM_BLOCK, LANES))
            shifted = pltpu.roll(src, 0, 1, stride=1, stride_axis=0)
            t_ref[0, c * SSM_BLOCK:(c + 1) * SSM_BLOCK, j * LANES:(j + 1) * LANES] = (
                jnp.where(causal, shifted, 0.0).astype(BF16))
    a_re, a_im = lam_pow(float(SSM_BLOCK), ar_r, ai_r, dt_r)
    are_ref[0] = a_re
    aim_ref[0] = a_im


def _ssm_prep(log_dt, a_re, a_im, b_re, b_im, c_re, c_im, d):
    g, p, cg = SSM_GROUPS, SSM_STATE, SSM_GROUP
    params = jnp.stack([a_re, a_im, jnp.broadcast_to(log_dt[:, None], (g, p))], axis=-1)
    pcol, prow = params, params.transpose(0, 2, 1)
    bcol = jnp.concatenate([b_re, b_im], axis=-1)
    brow = bcol.transpose(0, 2, 1)
    ccol = jnp.concatenate([c_re.transpose(0, 2, 1), c_im.transpose(0, 2, 1)], axis=-1)
    dcol = d.reshape(g, cg, 1)
    spec = lambda *shape: pl.BlockSpec((1,) + shape, lambda i: (i, 0, 0))
    n = SSM_BLOCK_DIM
    return pl.pallas_call(
        _ssm_prep_body,
        grid=(g,),
        in_specs=[spec(p, 3), spec(3, p), spec(p, 2 * cg), spec(2 * cg, p), spec(p, 2 * cg), spec(cg, 1)],
        out_specs=[spec(n, n), spec(p, n), spec(p, n), spec(p, n), spec(p, n), spec(1, p), spec(1, p)],
        out_shape=[jax.ShapeDtypeStruct((g, n, n), BF16)] + [jax.ShapeDtypeStruct((g, p, n), BF16)] * 4
                  + [jax.ShapeDtypeStruct((g, 1, p), F32)] * 2,
        compiler_params=_params(1),
        name="ssm_prep",
    )(pcol, prow, bcol, brow, ccol, dcol)


def _ssm_in_body(x_ref, w_ref, b_ref, o_ref, *, tm):
    xb = x_ref[...].reshape(tm, D_MODEL).astype(BF16)
    ut = lax.dot_general(w_ref[...], xb, (((1,), (1,)), ((), ())), preferred_element_type=F32)
    o_ref[...] = (ut + b_ref[...]).astype(o_ref.dtype)


def _ssm_in(x, x_spec, n_tiles, w_t, b_col, tm, out_dtype):
    return pl.pallas_call(
        functools.partial(_ssm_in_body, tm=tm),
        grid=(n_tiles,),
        in_specs=[x_spec, _const_spec((D_MODEL, D_MODEL)), _const_spec((D_MODEL, 1))],
        out_specs=pl.BlockSpec((D_MODEL, tm), lambda i: (0, i)),
        out_shape=jax.ShapeDtypeStruct((D_MODEL, n_tiles * tm), out_dtype),
        compiler_params=_params(1),
        name="ssm_in",
    )(x, w_t, b_col)


def _ssm_core_body(x_ref, xs_ref, s0re_ref, s0im_ref, t_ref, wtre_ref, wtim_ref, vre_ref, vim_ref, are_ref, aim_ref,
                   z_ref, zs_ref, sre_ref, sim_ref, pre_ref, pim_ref, sin_re, sin_im, *, half_rows, n_seq):
    n_prompt = 2 * half_rows
    n_rows = sin_re.shape[0]
    lo = lax.broadcasted_iota(jnp.int32, (half_rows, LANES), 1) < SSM_BLOCK
    lo_s = lax.broadcasted_iota(jnp.int32, (n_seq, LANES), 1) < SSM_BLOCK
    pad = jnp.zeros((n_rows - n_prompt - n_seq, LANES), F32)
    cols = []
    for j in range(SSM_GROUP // 2):
        xa = x_ref[2 * j].astype(F32)
        xb = x_ref[2 * j + 1].astype(F32)
        first = jnp.where(lo, xa, pltpu.roll(xb, SSM_BLOCK, 1))
        second = jnp.where(lo, pltpu.roll(xa, SSM_BLOCK, 1), xb)
        samp = jnp.where(lo_s, xs_ref[2 * j], pltpu.roll(xs_ref[2 * j + 1], SSM_BLOCK, 1))
        cols.append(jnp.concatenate([first, second, samp, pad], axis=0).astype(BF16))
    u = jnp.concatenate(cols, axis=1)

    nt = (((1,), (1,)), ((), ()))
    inj_re = lax.dot_general(u, wtre_ref[0], nt, preferred_element_type=F32)
    inj_im = lax.dot_general(u, wtim_ref[0], nt, preferred_element_type=F32)
    a_re, a_im = are_ref[0], aim_ref[0]

    s_re = jnp.zeros((n_seq, SSM_STATE), F32)
    s_im = jnp.zeros((n_seq, SSM_STATE), F32)
    for n in range(n_prompt // n_seq):
        r = slice(n * n_seq, (n + 1) * n_seq)
        sin_re[r, :] = s_re
        sin_im[r, :] = s_im
        s_re, s_im = (a_re * s_re - a_im * s_im + inj_re[r], a_re * s_im + a_im * s_re + inj_im[r])
    pre_ref[0] = s_re
    pim_ref[0] = s_im
    rs = slice(n_prompt, n_prompt + n_seq)
    s0_re, s0_im = s0re_ref[0], s0im_ref[0]
    sin_re[rs, :] = s0_re
    sin_im[rs, :] = s0_im
    sin_re[n_prompt + n_seq:, :] = jnp.zeros((n_rows - n_prompt - n_seq, SSM_STATE), F32)
    sin_im[n_prompt + n_seq:, :] = jnp.zeros((n_rows - n_prompt - n_seq, SSM_STATE), F32)
    sre_ref[0] = a_re * s0_re - a_im * s0_im + inj_re[rs]
    sim_ref[0] = a_re * s0_im + a_im * s0_re + inj_im[rs]

    y = (jnp.dot(u, t_ref[0], preferred_element_type=F32)
         + jnp.dot(sin_re[...].astype(BF16), vre_ref[0], preferred_element_type=F32)
         + jnp.dot(sin_im[...].astype(BF16), vim_ref[0], preferred_element_type=F32))
    z = jax.nn.gelu(y)
    for j in range(SSM_GROUP // 2):
        blk = z[:, j * LANES:(j + 1) * LANES]
        first, second = blk[:half_rows], blk[half_rows:n_prompt]
        z_ref[2 * j] = jnp.where(lo, first, pltpu.roll(second, SSM_BLOCK, 1)).astype(z_ref.dtype)
        z_ref[2 * j + 1] = jnp.where(lo, pltpu.roll(first, SSM_BLOCK, 1), second).astype(z_ref.dtype)
        samp = blk[rs]
        zs_ref[2 * j] = samp
        zs_ref[2 * j + 1] = pltpu.roll(samp, SSM_BLOCK, 1)


def _ssm_core(ut3, uts3, s0_re, s0_im, ops, *, half_rows, n_seq):
    t_op, wt_re, wt_im, v_re, v_im, a_re, a_im = ops
    g, p, cg, n = SSM_GROUPS, SSM_STATE, SSM_GROUP, SSM_BLOCK_DIM
    n_rows = 2 * half_rows + 2 * n_seq
    grp = lambda *shape: pl.BlockSpec((1,) + shape, lambda i: (i, 0, 0))
    chan = lambda rows: pl.BlockSpec((cg, rows, LANES), lambda i: (i, 0, 0))
    state = jax.ShapeDtypeStruct((g, n_seq, p), F32)
    return pl.pallas_call(
        functools.partial(_ssm_core_body, half_rows=half_rows, n_seq=n_seq),
        grid=(g,),
        in_specs=[chan(half_rows), chan(n_seq), grp(n_seq, p), grp(n_seq, p),
                  grp(n, n), grp(p, n), grp(p, n), grp(p, n), grp(p, n), grp(1, p), grp(1, p)],
        out_specs=[chan(half_rows), chan(n_seq), grp(n_seq, p), grp(n_seq, p), grp(n_seq, p), grp(n_seq, p)],
        out_shape=[jax.ShapeDtypeStruct(ut3.shape, BF16), jax.ShapeDtypeStruct(uts3.shape, F32),
                   state, state, state, state],
        scratch_shapes=[pltpu.VMEM((n_rows, p), F32), pltpu.VMEM((n_rows, p), F32)],
        compiler_params=_params(1),
        name="ssm_core",
    )(ut3, uts3, s0_re, s0_im, t_op, wt_re, wt_im, v_re, v_im, a_re, a_im)


def _glu_body(x_ref, zt_ref, w_ref, b_ref, g_ref, bb_ref, o_ref, *, tm):
    x = x_ref[...].reshape(tm, D_MODEL)
    gv = jnp.dot(w_ref[...], zt_ref[...], preferred_element_type=F32) + b_ref[...]
    mixed = gv[:D_MODEL] * jax.nn.sigmoid(gv[D_MODEL:])
    y = DN_ALPHA * x + mixed.T
    o_ref[...] = _layer_norm(y, g_ref[...], bb_ref[...]).reshape(o_ref.shape)


def _glu_ln(x, x_spec, n_tiles, zt, w_t, b_col, gain, bias, tm):
    return pl.pallas_call(
        functools.partial(_glu_body, tm=tm),
        grid=(n_tiles,),
        in_specs=[x_spec, pl.BlockSpec((D_MODEL, tm), lambda i: (0, i)),
                  _const_spec((2 * D_MODEL, D_MODEL)), _const_spec((2 * D_MODEL, 1)),
                  _const_spec((1, D_MODEL)), _const_spec((1, D_MODEL))],
        out_specs=x_spec,
        out_shape=jax.ShapeDtypeStruct(x.shape, F32),
        compiler_params=_params(1),
        name="glu_ln",
    )(x, zt, w_t, b_col, gain, bias)


def kernel(x_prompt, x_sample, cache_k, cache_v, state_ssm_re, state_ssm_im, attn_w_qkv, attn_b_qkv, attn_sinks, attn_w_o, attn_b_o, ssm_w_in, ssm_b_in, ssm_log_dt, ssm_a_re, ssm_a_im, ssm_b_re, ssm_b_im, ssm_c_re, ssm_c_im, ssm_d, ssm_w_glu, ssm_b_glu, ffn_w_up, ffn_w_down, ln_gain, ln_bias):
    n_b, seq, _ = x_prompt.shape
    n_s, seq_s, _ = x_sample.shape
    n_p, n_d = n_b * seq, n_s * seq_s
    tm = 512
    blocks = seq // SSM_BLOCK
    half_rows = (blocks // 2) * n_b
    ssm_tm = 2 * SSM_BLOCK * n_b

    xp = x_prompt.reshape(n_p, D_MODEL)
    xs = x_sample.reshape(n_d, D_MODEL)
    tabs_p = _rope_tables(jnp.arange(seq))
    tabs_s = tuple(jnp.tile(t, (n_s, 1)) for t in _rope_tables(PAST_LEN + jnp.arange(seq_s)))

    ssm_view = (n_b, 2, blocks // 2, SSM_BLOCK, D_MODEL)
    ssm_spec = pl.BlockSpec((n_b, 2, 1, SSM_BLOCK, D_MODEL), lambda i: (0, 0, i, 0, 0))
    flat_spec = pl.BlockSpec((n_d, D_MODEL), lambda i: (i, 0))

    new_k_p, new_v_p, new_re_p, new_im_p = [], [], [], []
    new_k_s, new_v_s, new_re_s, new_im_s = [], [], [], []
    for i in range(DEPTH):
        l = i // 2
        gain = ln_gain[i][:, None, :]
        bias = ln_bias[i][:, None, :]
        if i % 2 == 0:
            w_qkv = attn_w_qkv[l].astype(BF16)
            b_qkv = attn_b_qkv[l][None]
            w_o = attn_w_o[l].astype(BF16)
            b_o = attn_b_o[l][None]
            sink_cols = jnp.repeat(attn_sinks[l].astype(F32), CHUNK).reshape(N_KV_HEADS, GQA_GROUP * CHUNK, 1)
            qp, kp, vp = _qkv_rope(xp, w_qkv, b_qkv, tabs_p, tm, seq // tm)
            qs, ks, vs = _qkv_rope(xs, w_qkv, b_qkv, tabs_s, n_d, 1)
            tiles = seq // tm
            prev_p = lambda b, t: (jnp.maximum(b * (seq // WINDOW) + t * (tm // WINDOW) - 1, 0), 0)
            xp = _attn_proj_ln(xp, qp, kp, kp, vp, vp, sink_cols, w_o, b_o, gain[0], bias[0],
                               n_batch=n_b, tq=tm, prev_map=prev_p, masked=True)
            ck = cache_k[l].reshape(n_s * WINDOW, KV_DIM)
            cv = cache_v[l].reshape(n_s * WINDOW, KV_DIM)
            xs = _attn_proj_ln(xs, qs, ck, ks, cv, vs, sink_cols, w_o, b_o, gain[0], bias[0],
                               n_batch=n_s, tq=seq_s, prev_map=lambda b, t: (b, 0), masked=False)
            new_k_p.append(kp.reshape(n_b, seq, N_KV_HEADS, HEAD_DIM)[:, seq - WINDOW:])
            new_v_p.append(vp.reshape(n_b, seq, N_KV_HEADS, HEAD_DIM)[:, seq - WINDOW:])
            ks4 = ks.reshape(n_s, seq_s, N_KV_HEADS, HEAD_DIM)
            vs4 = vs.reshape(n_s, seq_s, N_KV_HEADS, HEAD_DIM)
            new_k_s.append(jnp.concatenate([cache_k[l], ks4], axis=1)[:, -WINDOW:])
            new_v_s.append(jnp.concatenate([cache_v[l], vs4], axis=1)[:, -WINDOW:])
        else:
            w_in_t = ssm_w_in[l].T.astype(BF16)
            b_in = ssm_b_in[l][:, None]
            w_glu_t = ssm_w_glu[l].T.astype(BF16)
            b_glu = ssm_b_glu[l][:, None]
            ops = _ssm_prep(ssm_log_dt[l], ssm_a_re[l], ssm_a_im[l], ssm_b_re[l], ssm_b_im[l],
                            ssm_c_re[l], ssm_c_im[l], ssm_d[l])
            ut = _ssm_in(xp.reshape(ssm_view), ssm_spec, blocks // 2, w_in_t, b_in, ssm_tm, BF16)
            ut3 = ut.reshape(D_MODEL, half_rows, LANES)
            uts = _ssm_in(xs, flat_spec, 1, w_in_t, b_in, n_d, F32)
            uts3 = jnp.pad(uts.reshape(D_MODEL, n_s, seq_s), ((0, 0), (0, 0), (0, LANES - seq_s)))
            s0_re = state_ssm_re[l].transpose(1, 0, 2)
            s0_im = state_ssm_im[l].transpose(1, 0, 2)
            z3, zs3, s_re, s_im, p_re, p_im = _ssm_core(ut3, uts3, s0_re, s0_im, ops, half_rows=half_rows, n_seq=n_s)
            zt = z3.reshape(D_MODEL, n_p)
            zst = zs3[:, :, :seq_s].reshape(D_MODEL, n_d).astype(BF16)
            xp = _glu_ln(xp.reshape(ssm_view), ssm_spec, blocks // 2, zt, w_glu_t, b_glu, gain[0], bias[0],
                         ssm_tm).reshape(n_p, D_MODEL)
            xs = _glu_ln(xs, flat_spec, 1, zst, w_glu_t, b_glu, gain[0], bias[0], n_d)
            new_re_p.append(p_re.transpose(1, 0, 2))
            new_im_p.append(p_im.transpose(1, 0, 2))
            new_re_s.append(s_re.transpose(1, 0, 2))
            new_im_s.append(s_im.transpose(1, 0, 2))
        w_up = ffn_w_up[i].astype(BF16).reshape(D_MODEL, 2, N_FF_CHUNKS, FF_CHUNK)
        w_up = w_up.transpose(0, 2, 1, 3).reshape(D_MODEL, 2 * D_FF)
        w_dn = ffn_w_down[i].astype(BF16)
        xp = _ffn_ln(xp, w_up, w_dn, gain[1], bias[1], tm)
        xs = _ffn_ln(xs, w_up, w_dn, gain[1], bias[1], n_d)
    return (xp.reshape(x_prompt.shape), xs.reshape(x_sample.shape),
            jnp.stack(new_k_p), jnp.stack(new_v_p), jnp.stack(new_re_p), jnp.stack(new_im_p),
            jnp.stack(new_k_s), jnp.stack(new_v_s), jnp.stack(new_re_s), jnp.stack(new_im_s))
```

```python
import functools

import jax
import jax.numpy as jnp
from jax import lax
from jax.experimental import pallas as pl
from jax.experimental.pallas import tpu as pltpu

F32 = jnp.float32
BF16 = jnp.bfloat16

D_MODEL = 1024
DEPTH = 4
CHUNK = 64
N_HEADS = 16
N_KV_HEADS = 2
HEAD_DIM = 64
GQA_GROUP = N_HEADS // N_KV_HEADS
Q_DIM = N_HEADS * HEAD_DIM
KV_DIM = N_KV_HEADS * HEAD_DIM
QKV_DIM = Q_DIM + 2 * KV_DIM
WINDOW = 128
PAST_LEN = 4096
ROPE_THETA = 10000.0
SSM_GROUP = 16
SSM_GROUPS = D_MODEL // SSM_GROUP
SSM_STATE = 64
D_FF = 2816
DN_ALPHA = (2.0 * DEPTH) ** 0.25
LN_EPS = 1e-5
NEG_INF = -1e30
LOG2E = 1.4426950408889634

LANES = 128
BF16_SUBLANES = 16
FF_CHUNK = 256
N_FF_CHUNKS = D_FF // FF_CHUNK
SSM_BLOCK = CHUNK
SSM_BLOCK_DIM = SSM_GROUP * SSM_BLOCK
VMEM_LIMIT = 56 * 2 ** 20


def _params(n_axes):
    return pltpu.CompilerParams(dimension_semantics=("arbitrary",) * n_axes, vmem_limit_bytes=VMEM_LIMIT)


def _const_spec(shape):
    zeros = (0,) * len(shape)
    return pl.BlockSpec(shape, lambda *_: zeros, pipeline_mode=pl.Buffered(1))


def _layer_norm(y, gain, bias):
    mu = jnp.mean(y, axis=-1, keepdims=True)
    d = y - mu
    var = jnp.mean(d * d, axis=-1, keepdims=True)
    return d * lax.rsqrt(var + LN_EPS) * gain + bias


def _ffn_body(x_ref, wup_ref, wdn_ref, g_ref, b_ref, o_ref):
    x = x_ref[...]
    xb = x.astype(BF16)
    acc = DN_ALPHA * x
    for c in range(N_FF_CHUNKS):
        cols = slice(FF_CHUNK * c, FF_CHUNK * (c + 1))
        gate = jnp.dot(xb, wup_ref[:, cols], preferred_element_type=F32)
        up = jnp.dot(xb, wup_ref[:, D_FF + FF_CHUNK * c:D_FF + FF_CHUNK * (c + 1)], preferred_element_type=F32)
        act = gate * jax.nn.sigmoid(gate) * up
        acc = acc + jnp.dot(act.astype(BF16), wdn_ref[cols, :], preferred_element_type=F32)
    o_ref[...] = _layer_norm(acc, g_ref[...], b_ref[...])


def _ffn_ln(x, wup, wdn, gain, bias, tm):
    n = x.shape[0]
    return pl.pallas_call(
        _ffn_body,
        grid=(n // tm,),
        in_specs=[pl.BlockSpec((tm, D_MODEL), lambda i: (i, 0)),
                  _const_spec((D_MODEL, 2 * D_FF)), _const_spec((D_FF, D_MODEL)),
                  _const_spec((1, D_MODEL)), _const_spec((1, D_MODEL))],
        out_specs=pl.BlockSpec((tm, D_MODEL), lambda i: (i, 0)),
        out_shape=jax.ShapeDtypeStruct((n, D_MODEL), F32),
        compiler_params=_params(1),
        name="ffn_ln",
    )(x, wup, wdn, gain, bias)


def _qkv_body(x_ref, w_ref, b_ref, cos_ref, sa_ref, sb_ref, q_ref, k_ref, v_ref):
    xb = x_ref[...].astype(BF16)
    qkv = jnp.dot(xb, w_ref[...], preferred_element_type=F32) + b_ref[...]
    cos, sin_a, sin_b = cos_ref[...], sa_ref[...], sb_ref[...]
    for s in range((Q_DIM + KV_DIM) // LANES):
        t = qkv[:, s * LANES:(s + 1) * LANES]
        r = t * cos + pltpu.roll(t, LANES - HEAD_DIM // 2, 1) * sin_a + pltpu.roll(t, HEAD_DIM // 2, 1) * sin_b
        if s < Q_DIM // LANES:
            q_ref[:, s * LANES:(s + 1) * LANES] = (r * (HEAD_DIM ** -0.5 * LOG2E)).astype(BF16)
        else:
            k_ref[...] = r
    v_ref[...] = qkv[:, Q_DIM + KV_DIM:]


def _qkv_rope(x, w, b, tabs, tm, tab_tiles):
    n = x.shape[0]
    tab_spec = pl.BlockSpec((tm, LANES), lambda i: (i % tab_tiles, 0))
    return pl.pallas_call(
        _qkv_body,
        grid=(n // tm,),
        in_specs=[pl.BlockSpec((tm, D_MODEL), lambda i: (i, 0)),
                  _const_spec((D_MODEL, QKV_DIM)), _const_spec((1, QKV_DIM)),
                  tab_spec, tab_spec, tab_spec],
        out_specs=[pl.BlockSpec((tm, Q_DIM), lambda i: (i, 0)),
                   pl.BlockSpec((tm, KV_DIM), lambda i: (i, 0)),
                   pl.BlockSpec((tm, KV_DIM), lambda i: (i, 0))],
        out_shape=[jax.ShapeDtypeStruct((n, Q_DIM), BF16),
                   jax.ShapeDtypeStruct((n, KV_DIM), F32),
                   jax.ShapeDtypeStruct((n, KV_DIM), F32)],
        compiler_params=_params(1),
        name="qkv_rope",
    )(x, w, b, *tabs)


def _rope_tables(pos):
    half = HEAD_DIM // 2
    lane = jnp.arange(LANES)
    inv = ROPE_THETA ** (-(lane % half).astype(F32) / half)
    ang = pos.astype(F32)[:, None] * inv[None, :]
    first = (lane % HEAD_DIM) < half
    sin = jnp.sin(ang)
    return jnp.cos(ang), jnp.where(first, -sin, 0.0), jnp.where(first, 0.0, sin)


PAIR = 2 * CHUNK


def _attn_body(x_ref, q_ref, kp_ref, kc_ref, vp_ref, vc_ref, sink_ref, wot_ref, bo_ref, g_ref, b_ref,
               out_ref, ot_scr, *, tq, has_start):
    span = WINDOW + PAIR
    kcat = jnp.concatenate([kp_ref[...], kc_ref[...]], axis=0)
    vcat = jnp.concatenate([vp_ref[...], vc_ref[...]], axis=0)
    lo_kv = lax.broadcasted_iota(jnp.int32, kcat.shape, 1) < HEAD_DIM
    k_sw = pltpu.roll(kcat, HEAD_DIM, 1)
    k_dup = [jnp.where(lo_kv, kcat, k_sw).astype(BF16), jnp.where(lo_kv, k_sw, kcat).astype(BF16)]
    v_t = vcat.T.astype(BF16)
    ones_rows = jnp.ones((BF16_SUBLANES, v_t.shape[1]), BF16)
    v_aug = [jnp.concatenate([v_t[m * HEAD_DIM:(m + 1) * HEAD_DIM], ones_rows], axis=0) for m in range(N_KV_HEADS)]
    lo_q = lax.broadcasted_iota(jnp.int32, (PAIR, LANES), 1) < HEAD_DIM
    n_cols = GQA_GROUP * PAIR
    tok = lax.broadcasted_iota(jnp.int32, (CHUNK, n_cols), 1) & (PAIR - 1)
    first_chunk = tok < CHUNK
    if has_start:
        after_start = jnp.broadcast_to(pl.program_id(1) > 0, first_chunk.shape)
    slabs_per_kv = GQA_GROUP * HEAD_DIM // LANES
    for pr in range(tq // PAIR):
        rows = slice(pr * PAIR, (pr + 1) * PAIR)
        for m in range(N_KV_HEADS):
            parts = []
            for jj in range(slabs_per_kv):
                col = (slabs_per_kv * m + jj) * LANES
                qs = q_ref[rows, col:col + LANES]
                parts.append(jnp.where(lo_q, qs, jnp.zeros_like(qs)))
                parts.append(jnp.where(lo_q, jnp.zeros_like(qs), qs))
            q_st = jnp.concatenate(parts, axis=0)
            kk = k_dup[m][pr * PAIR:pr * PAIR + span]
            s = lax.dot_general(kk, q_st, (((1,), (1,)), ((), ())), preferred_element_type=F32)
            oldest, older = s[:CHUNK], s[CHUNK:2 * CHUNK]
            if has_start and pr == 0:
                oldest = jnp.where(after_start, oldest, NEG_INF)
                older = jnp.where(after_start, older, NEG_INF)
            edge = jnp.where(first_chunk, oldest, s[3 * CHUNK:])
            s = jnp.concatenate([edge, older, s[2 * CHUNK:3 * CHUNK]], axis=0)
            sink = sink_ref[m] * LOG2E
            mx = jnp.maximum(jnp.max(s, axis=0, keepdims=True), sink)
            e = jnp.exp2(s - mx)
            e_edge = e[:CHUNK]
            p_t = jnp.concatenate([jnp.where(first_chunk, e_edge, 0.0), e[CHUNK:],
                                   jnp.where(first_chunk, 0.0, e_edge)], axis=0).astype(BF16)
            o_aug = jnp.dot(v_aug[m][:, pr * PAIR:pr * PAIR + span], p_t, preferred_element_type=F32)
            den = o_aug[HEAD_DIM:HEAD_DIM + 1] + jnp.exp2(sink - mx)
            o_t = o_aug[:HEAD_DIM] * (1.0 / den)
            for hb in range(GQA_GROUP):
                head = GQA_GROUP * m + hb
                ot_scr[head * HEAD_DIM:(head + 1) * HEAD_DIM, rows] = o_t[:, hb * PAIR:(hb + 1) * PAIR].astype(BF16)
    proj_t = jnp.dot(wot_ref[...], ot_scr[...], preferred_element_type=F32)
    y = DN_ALPHA * x_ref[...] + proj_t.T + bo_ref[...]
    out_ref[...] = _layer_norm(y, g_ref[...], b_ref[...])


def _attn_proj_ln(x, q, k_prev, k_cur, v_prev, v_cur, sink_rows, wo_t, bo, gain, bias, *, n_batch, tq, prev_map, has_start):
    n = x.shape[0]
    tiles = n // (n_batch * tq)
    cur = lambda b, i: (b * tiles + i, 0)
    return pl.pallas_call(
        functools.partial(_attn_body, tq=tq, has_start=has_start),
        grid=(n_batch, tiles),
        in_specs=[pl.BlockSpec((tq, D_MODEL), cur), pl.BlockSpec((tq, Q_DIM), cur),
                  pl.BlockSpec((WINDOW, KV_DIM), prev_map), pl.BlockSpec((tq, KV_DIM), cur),
                  pl.BlockSpec((WINDOW, KV_DIM), prev_map), pl.BlockSpec((tq, KV_DIM), cur),
                  _const_spec((N_KV_HEADS, 1, GQA_GROUP * PAIR)),
                  _const_spec((D_MODEL, Q_DIM)), _const_spec((1, D_MODEL)),
                  _const_spec((1, D_MODEL)), _const_spec((1, D_MODEL))],
        out_specs=pl.BlockSpec((tq, D_MODEL), cur),
        out_shape=jax.ShapeDtypeStruct((n, D_MODEL), F32),
        scratch_shapes=[pltpu.VMEM((Q_DIM, tq), BF16)],
        compiler_params=_params(2),
        name="attn_proj_ln",
    )(x, q, k_prev, k_cur, v_prev, v_cur, sink_rows, wo_t, bo, gain, bias)


def _ssm_prep_body(pcol_ref, prow_ref, bcol_ref, brow_ref, ccol_ref, dcol_ref, t_ref, wt_ref, v_ref, are_ref, aim_ref):
    def cmul(x_re, x_im, y_re, y_im):
        return x_re * y_re - x_im * y_im, x_re * y_im + x_im * y_re

    def a_bar(a_re, a_im, dt):
        mag = jnp.exp(dt * a_re)
        ang = dt * a_im
        return mag * jnp.cos(ang), mag * jnp.sin(ang)

    def zoh_factor(l_re, l_im, a_re, a_im):
        den = a_re * a_re + a_im * a_im
        n_re = l_re - 1.0
        return (n_re * a_re + l_im * a_im) / den, (l_im * a_re - n_re * a_im) / den

    n_bits = SSM_BLOCK.bit_length() - 1
    pc = pcol_ref[0]
    ar_c, ai_c, dt_c = pc[:, 0:1], pc[:, 1:2], jnp.exp(pc[:, 2:3])
    lam_c = a_bar(ar_c, ai_c, dt_c)
    fr_c, fi_c = zoh_factor(*lam_c, ar_c, ai_c)
    bc = bcol_ref[0]
    bbr_c = fr_c * bc[:, :SSM_GROUP] - fi_c * bc[:, SSM_GROUP:]
    bbi_c = fr_c * bc[:, SSM_GROUP:] + fi_c * bc[:, :SSM_GROUP]
    pr = prow_ref[0]
    ar_r, ai_r, dt_r = pr[0:1], pr[1:2], jnp.exp(pr[2:3])
    lam_r = a_bar(ar_r, ai_r, dt_r)
    fr_r, fi_r = zoh_factor(*lam_r, ar_r, ai_r)
    br = brow_ref[0]
    bbr_r = fr_r * br[:SSM_GROUP] - fi_r * br[SSM_GROUP:]
    bbi_r = fr_r * br[SSM_GROUP:] + fi_r * br[:SSM_GROUP]
    cc = ccol_ref[0]

    squares = [lam_c]
    for _ in range(n_bits - 1):
        squares.append(cmul(*squares[-1], *squares[-1]))

    def lane_pow(exps):
        acc = None
        for k, (q_re, q_im) in enumerate(squares):
            bit = ((exps >> k) & 1) == 1
            f_re, f_im = jnp.where(bit, q_re, 1.0), jnp.where(bit, q_im, 0.0)
            acc = (f_re, f_im) if acc is None else cmul(*acc, f_re, f_im)
        return acc

    lane = lax.broadcasted_iota(jnp.int32, (SSM_STATE, LANES), 1)
    lo = lane < SSM_BLOCK
    tau = lane & (SSM_BLOCK - 1)
    l0r, l0i = lane_pow(tau)
    l1r, l1i = cmul(l0r, l0i, *lam_c)
    lrr, lri = lane_pow(SSM_BLOCK - 1 - tau)

    g_re, g_im = [], []
    for j in range(SSM_GROUP // 2):
        sl = slice(j * LANES, (j + 1) * LANES)
        cr = jnp.where(lo, cc[:, 2 * j:2 * j + 1], cc[:, 2 * j + 1:2 * j + 2])
        ci = jnp.where(lo, cc[:, SSM_GROUP + 2 * j:SSM_GROUP + 2 * j + 1],
                       cc[:, SSM_GROUP + 2 * j + 1:SSM_GROUP + 2 * j + 2])
        g_re.append(cr * l0r - ci * l0i)
        g_im.append(-(cr * l0i + ci * l0r))
        v_ref[0, :SSM_STATE, sl] = (cr * l1r - ci * l1i).astype(BF16)
        v_ref[0, SSM_STATE:, sl] = (-(cr * l1i + ci * l1r)).astype(BF16)
        b_r = jnp.where(lo, bbr_c[:, 2 * j:2 * j + 1], bbr_c[:, 2 * j + 1:2 * j + 2])
        b_i = jnp.where(lo, bbi_c[:, 2 * j:2 * j + 1], bbi_c[:, 2 * j + 1:2 * j + 2])
        wt_ref[0, :SSM_STATE, sl] = (b_r * lrr - b_i * lri).astype(BF16)
        wt_ref[0, SSM_STATE:, sl] = (b_r * lri + b_i * lrr).astype(BF16)
    g_re = jnp.concatenate(g_re, axis=1)
    g_im = jnp.concatenate(g_im, axis=1)
    kvec = (jnp.dot(bbr_r, g_re, preferred_element_type=F32, precision=lax.Precision.HIGHEST)
            + jnp.dot(bbi_r, g_im, preferred_element_type=F32, precision=lax.Precision.HIGHEST))
    klane = lax.broadcasted_iota(jnp.int32, kvec.shape, 1)
    krow = lax.broadcasted_iota(jnp.int32, kvec.shape, 0)
    kvec = kvec + jnp.where(klane == krow * SSM_BLOCK, dcol_ref[0], 0.0)

    t_in = lax.broadcasted_iota(jnp.int32, (SSM_BLOCK, LANES), 0)
    causal = (lax.broadcasted_iota(jnp.int32, (SSM_BLOCK, LANES), 1) & (SSM_BLOCK - 1)) >= t_in
    for c in range(SSM_GROUP):
        for j in range(SSM_GROUP // 2):
            src = jnp.broadcast_to(kvec[c:c + 1, j * LANES:(j + 1) * LANES], (SSM_BLOCK, LANES))
            shifted = pltpu.roll(src, 0, 1, stride=1, stride_axis=0)
            t_ref[0, c * SSM_BLOCK:(c + 1) * SSM_BLOCK, j * LANES:(j + 1) * LANES] = (
                jnp.where(causal, shifted, 0.0).astype(BF16))
    a_re, a_im = lam_r
    for _ in range(n_bits):
        a_re, a_im = cmul(a_re, a_im, a_re, a_im)
    are_ref[0] = a_re
    aim_ref[0] = a_im


def _ssm_prep(log_dt, a_re, a_im, b_re, b_im, c_re, c_im, d):
    g, p, cg = SSM_GROUPS, SSM_STATE, SSM_GROUP
    params = jnp.stack([a_re, a_im, jnp.broadcast_to(log_dt[:, None], (g, p))], axis=-1)
    pcol, prow = params, params.transpose(0, 2, 1)
    bcol = jnp.concatenate([b_re, b_im], axis=-1)
    brow = bcol.transpose(0, 2, 1)
    ccol = jnp.concatenate([c_re.transpose(0, 2, 1), c_im.transpose(0, 2, 1)], axis=-1)
    dcol = d.reshape(g, cg, 1)
    spec = lambda *shape: pl.BlockSpec((1,) + shape, lambda i: (i, 0, 0))
    n = SSM_BLOCK_DIM
    return pl.pallas_call(
        _ssm_prep_body,
        grid=(g,),
        in_specs=[spec(p, 3), spec(3, p), spec(p, 2 * cg), spec(2 * cg, p), spec(p, 2 * cg), spec(cg, 1)],
        out_specs=[spec(n, n), spec(2 * p, n), spec(2 * p, n), spec(1, p), spec(1, p)],
        out_shape=[jax.ShapeDtypeStruct((g, n, n), BF16)] + [jax.ShapeDtypeStruct((g, 2 * p, n), BF16)] * 2
                  + [jax.ShapeDtypeStruct((g, 1, p), F32)] * 2,
        compiler_params=_params(1),
        name="ssm_prep",
    )(pcol, prow, bcol, brow, ccol, dcol)


def _ssm_in_body(x_ref, w_ref, b_ref, o_ref, *, tm):
    xb = x_ref[...].reshape(tm, D_MODEL).astype(BF16)
    ut = lax.dot_general(w_ref[...], xb, (((1,), (1,)), ((), ())), preferred_element_type=F32)
    ut = ut + b_ref[...]
    if len(o_ref.shape) == 2:
        o_ref[...] = ut
    else:
        for r in range(o_ref.shape[1]):
            o_ref[:, r, :] = ut[:, r * LANES:(r + 1) * LANES]


def _ssm_in(x, x_spec, n_tiles, w_t, b_col, tm, planes):
    if planes:
        out_spec = pl.BlockSpec((D_MODEL, tm // LANES, LANES), lambda i: (0, i, 0))
        out_shape = jax.ShapeDtypeStruct((D_MODEL, n_tiles * tm // LANES, LANES), F32)
    else:
        out_spec = pl.BlockSpec((D_MODEL, tm), lambda i: (0, i))
        out_shape = jax.ShapeDtypeStruct((D_MODEL, n_tiles * tm), F32)
    return pl.pallas_call(
        functools.partial(_ssm_in_body, tm=tm),
        grid=(n_tiles,),
        in_specs=[x_spec, _const_spec((D_MODEL, D_MODEL)), _const_spec((D_MODEL, 1))],
        out_specs=out_spec,
        out_shape=out_shape,
        compiler_params=_params(1),
        name="ssm_in",
    )(x, w_t, b_col)


def _ssm_core_body(x_ref, xs_ref, s0re_ref, s0im_ref, t_ref, wt_ref, v_ref, are_ref, aim_ref,
                   z_ref, zs_ref, sre_ref, sim_ref, pre_ref, pim_ref, s_in, *, half_rows, n_seq):
    n_prompt = 2 * half_rows
    n_rows = s_in.shape[0]
    lo = lax.broadcasted_iota(jnp.int32, (half_rows, LANES), 1) < SSM_BLOCK
    lo_s = lax.broadcasted_iota(jnp.int32, (n_seq, LANES), 1) < SSM_BLOCK
    pad = jnp.zeros((n_rows - n_prompt - n_seq, LANES), F32)
    cols = []
    for j in range(SSM_GROUP // 2):
        xa = x_ref[2 * j]
        xb = x_ref[2 * j + 1]
        first = jnp.where(lo, xa, pltpu.roll(xb, SSM_BLOCK, 1))
        second = jnp.where(lo, pltpu.roll(xa, SSM_BLOCK, 1), xb)
        samp = jnp.where(lo_s, xs_ref[2 * j], pltpu.roll(xs_ref[2 * j + 1], SSM_BLOCK, 1))
        cols.append(jnp.concatenate([first, second, samp, pad], axis=0).astype(BF16))
    u = jnp.concatenate(cols, axis=1)

    inj = lax.dot_general(u, wt_ref[0], (((1,), (1,)), ((), ())), preferred_element_type=F32)
    inj_re, inj_im = inj[:, :SSM_STATE], inj[:, SSM_STATE:]
    y_u = jnp.dot(u, t_ref[0], preferred_element_type=F32)
    a_re, a_im = are_ref[0], aim_ref[0]

    s_re = jnp.zeros((n_seq, SSM_STATE), F32)
    s_im = jnp.zeros((n_seq, SSM_STATE), F32)
    for n in range(n_prompt // n_seq):
        r = slice(n * n_seq, (n + 1) * n_seq)
        s_in[r, :SSM_STATE] = s_re
        s_in[r, SSM_STATE:] = s_im
        s_re, s_im = (a_re * s_re - a_im * s_im + inj_re[r], a_re * s_im + a_im * s_re + inj_im[r])
    pre_ref[0] = s_re
    pim_ref[0] = s_im
    rs = slice(n_prompt, n_prompt + n_seq)
    s0_re, s0_im = s0re_ref[0], s0im_ref[0]
    s_in[rs, :SSM_STATE] = s0_re
    s_in[rs, SSM_STATE:] = s0_im
    s_in[n_prompt + n_seq:, :] = jnp.zeros((n_rows - n_prompt - n_seq, 2 * SSM_STATE), F32)
    sre_ref[0] = a_re * s0_re - a_im * s0_im + inj_re[rs]
    sim_ref[0] = a_re * s0_im + a_im * s0_re + inj_im[rs]

    s_bf = s_in[...].astype(BF16)
    width = 2 * LANES
    for h in range(SSM_BLOCK_DIM // width):
        y = y_u[:, h * width:(h + 1) * width] + jnp.dot(s_bf, v_ref[0, :, h * width:(h + 1) * width],
                                                        preferred_element_type=F32)
        z = jax.nn.gelu(y)
        for jj in range(width // LANES):
            j = h * (width // LANES) + jj
            blk = z[:, jj * LANES:(jj + 1) * LANES]
            first, second = blk[:half_rows], blk[half_rows:n_prompt]
            z_ref[2 * j] = jnp.where(lo, first, pltpu.roll(second, SSM_BLOCK, 1)).astype(z_ref.dtype)
            z_ref[2 * j + 1] = jnp.where(lo, pltpu.roll(first, SSM_BLOCK, 1), second).astype(z_ref.dtype)
            samp = blk[rs]
            zs_ref[2 * j] = samp
            zs_ref[2 * j + 1] = pltpu.roll(samp, SSM_BLOCK, 1)


def _ssm_core(ut3, uts3, s0_re, s0_im, ops, *, half_rows, n_seq):
    t_op, wt, v_op, a_re, a_im = ops
    g, p, cg, n = SSM_GROUPS, SSM_STATE, SSM_GROUP, SSM_BLOCK_DIM
    n_rows = 2 * half_rows + 2 * n_seq
    grp = lambda *shape: pl.BlockSpec((1,) + shape, lambda i: (i, 0, 0))
    chan = lambda rows: pl.BlockSpec((cg, rows, LANES), lambda i: (i, 0, 0))
    state = jax.ShapeDtypeStruct((g, n_seq, p), F32)
    return pl.pallas_call(
        functools.partial(_ssm_core_body, half_rows=half_rows, n_seq=n_seq),
        grid=(g,),
        in_specs=[chan(half_rows), chan(n_seq), grp(n_seq, p), grp(n_seq, p),
                  grp(n, n), grp(2 * p, n), grp(2 * p, n), grp(1, p), grp(1, p)],
        out_specs=[chan(half_rows), chan(n_seq), grp(n_seq, p), grp(n_seq, p), grp(n_seq, p), grp(n_seq, p)],
        out_shape=[jax.ShapeDtypeStruct(ut3.shape, BF16), jax.ShapeDtypeStruct(uts3.shape, F32),
                   state, state, state, state],
        scratch_shapes=[pltpu.VMEM((n_rows, 2 * p), F32)],
        compiler_params=_params(1),
        name="ssm_core",
    )(ut3, uts3, s0_re, s0_im, t_op, wt, v_op, a_re, a_im)


def _glu_body(x_ref, zt_ref, w_ref, b_ref, g_ref, bb_ref, o_ref, *, tm):
    x = x_ref[...].reshape(tm, D_MODEL)
    gv = jnp.dot(w_ref[...], zt_ref[...], preferred_element_type=F32) + b_ref[...]
    mixed = gv[:D_MODEL] * jax.nn.sigmoid(gv[D_MODEL:])
    y = DN_ALPHA * x + mixed.T
    o_ref[...] = _layer_norm(y, g_ref[...], bb_ref[...]).reshape(o_ref.shape)


def _glu_ln(x, x_spec, n_tiles, zt, w_t, b_col, gain, bias, tm):
    return pl.pallas_call(
        functools.partial(_glu_body, tm=tm),
        grid=(n_tiles,),
        in_specs=[x_spec, pl.BlockSpec((D_MODEL, tm), lambda i: (0, i)),
                  _const_spec((2 * D_MODEL, D_MODEL)), _const_spec((2 * D_MODEL, 1)),
                  _const_spec((1, D_MODEL)), _const_spec((1, D_MODEL))],
        out_specs=x_spec,
        out_shape=jax.ShapeDtypeStruct(x.shape, F32),
        compiler_params=_params(1),
        name="glu_ln",
    )(x, zt, w_t, b_col, gain, bias)


def kernel(x_prompt, x_sample, cache_k, cache_v, state_ssm_re, state_ssm_im, attn_w_qkv, attn_b_qkv, attn_sinks, attn_w_o, attn_b_o, ssm_w_in, ssm_b_in, ssm_log_dt, ssm_a_re, ssm_a_im, ssm_b_re, ssm_b_im, ssm_c_re, ssm_c_im, ssm_d, ssm_w_glu, ssm_b_glu, ffn_w_up, ffn_w_down, ln_gain, ln_bias):
    n_b, seq, _ = x_prompt.shape
    n_s, seq_s, _ = x_sample.shape
    n_p, n_d = n_b * seq, n_s * seq_s
    tm = 512
    blocks = seq // SSM_BLOCK
    half_rows = (blocks // 2) * n_b
    ssm_tm = 2 * SSM_BLOCK * n_b

    xp = x_prompt.reshape(n_p, D_MODEL)
    xs = x_sample.reshape(n_d, D_MODEL)
    tabs_p = _rope_tables(jnp.arange(seq))
    tabs_s = tuple(jnp.tile(t, (n_s, 1)) for t in _rope_tables(PAST_LEN + jnp.arange(seq_s)))

    ssm_view = (n_b, 2, blocks // 2, SSM_BLOCK, D_MODEL)
    ssm_spec = pl.BlockSpec((n_b, 2, 1, SSM_BLOCK, D_MODEL), lambda i: (0, 0, i, 0, 0))
    flat_spec = pl.BlockSpec((n_d, D_MODEL), lambda i: (i, 0))

    new_k_p, new_v_p, new_re_p, new_im_p = [], [], [], []
    new_k_s, new_v_s, new_re_s, new_im_s = [], [], [], []
    for i in range(DEPTH):
        l = i // 2
        gain = ln_gain[i][:, None, :]
        bias = ln_bias[i][:, None, :]
        if i % 2 == 0:
            w_qkv = attn_w_qkv[l].astype(BF16)
            b_qkv = attn_b_qkv[l][None]
            w_o_t = attn_w_o[l].T.astype(BF16)
            b_o = attn_b_o[l][None]
            sink_rows = jnp.repeat(attn_sinks[l].astype(F32), PAIR).reshape(N_KV_HEADS, 1, GQA_GROUP * PAIR)
            qp, kp, vp = _qkv_rope(xp, w_qkv, b_qkv, tabs_p, tm, seq // tm)
            qs, ks, vs = _qkv_rope(xs, w_qkv, b_qkv, tabs_s, n_d, 1)
            prev_p = lambda b, t: (jnp.maximum(b * (seq // WINDOW) + t * (tm // WINDOW) - 1, 0), 0)
            xp = _attn_proj_ln(xp, qp, kp, kp, vp, vp, sink_rows, w_o_t, b_o, gain[0], bias[0],
                               n_batch=n_b, tq=tm, prev_map=prev_p, has_start=True)
            pad_pair = lambda a: jnp.pad(a.reshape(n_s, seq_s, -1), ((0, 0), (0, PAIR - seq_s), (0, 0))).reshape(n_s * PAIR, -1)
            ck = cache_k[l].reshape(n_s * WINDOW, KV_DIM)
            cv = cache_v[l].reshape(n_s * WINDOW, KV_DIM)
            xs = _attn_proj_ln(pad_pair(xs), pad_pair(qs), ck, pad_pair(ks), cv, pad_pair(vs), sink_rows, w_o_t, b_o,
                               gain[0], bias[0], n_batch=n_s, tq=PAIR, prev_map=lambda b, t: (b, 0), has_start=False)
            xs = xs.reshape(n_s, PAIR, D_MODEL)[:, :seq_s].reshape(n_d, D_MODEL)
            last = lambda a: a.reshape(n_b, seq, KV_DIM)[:, seq - WINDOW:].reshape(n_b, WINDOW, N_KV_HEADS, HEAD_DIM)
            new_k_p.append(last(kp))
            new_v_p.append(last(vp))
            ks4 = ks.reshape(n_s, seq_s, N_KV_HEADS, HEAD_DIM)
            vs4 = vs.reshape(n_s, seq_s, N_KV_HEADS, HEAD_DIM)
            new_k_s.append(jnp.concatenate([cache_k[l], ks4], axis=1)[:, -WINDOW:])
            new_v_s.append(jnp.concatenate([cache_v[l], vs4], axis=1)[:, -WINDOW:])
        else:
            w_in_t = ssm_w_in[l].T.astype(BF16)
            b_in = ssm_b_in[l][:, None]
            w_glu_t = ssm_w_glu[l].T.astype(BF16)
            b_glu = ssm_b_glu[l][:, None]
            ops = _ssm_prep(ssm_log_dt[l], ssm_a_re[l], ssm_a_im[l], ssm_b_re[l], ssm_b_im[l],
                            ssm_c_re[l], ssm_c_im[l], ssm_d[l])
            ut3 = _ssm_in(xp.reshape(ssm_view), ssm_spec, blocks // 2, w_in_t, b_in, ssm_tm, True)
            uts = _ssm_in(xs, flat_spec, 1, w_in_t, b_in, n_d, False)
            uts3 = jnp.pad(uts.reshape(D_MODEL, n_s, seq_s), ((0, 0), (0, 0), (0, LANES - seq_s)))
            s0_re = state_ssm_re[l].transpose(1, 0, 2)
            s0_im = state_ssm_im[l].transpose(1, 0, 2)
            z3, zs3, s_re, s_im, p_re, p_im = _ssm_core(ut3, uts3, s0_re, s0_im, ops, half_rows=half_rows, n_seq=n_s)
            zst = zs3[:, :, :seq_s].reshape(D_MODEL, n_d).astype(BF16)
            xp = _glu_ln(xp.reshape(ssm_view), ssm_spec, blocks // 2, z3.reshape(D_MODEL, n_p), w_glu_t, b_glu, gain[0], bias[0],
                         ssm_tm).reshape(n_p, D_MODEL)
            xs = _glu_ln(xs, flat_spec, 1, zst, w_glu_t, b_glu, gain[0], bias[0], n_d)
            new_re_p.append(p_re.transpose(1, 0, 2))
            new_im_p.append(p_im.transpose(1, 0, 2))
            new_re_s.append(s_re.transpose(1, 0, 2))
            new_im_s.append(s_im.transpose(1, 0, 2))
        w_up = ffn_w_up[i].astype(BF16)
        w_dn = ffn_w_down[i].astype(BF16)
        xp = _ffn_ln(xp, w_up, w_dn, gain[1], bias[1], tm)
        xs = _ffn_ln(xs, w_up, w_dn, gain[1], bias[1], n_d)
    return (xp.reshape(x_prompt.shape), xs.reshape(x_sample.shape),
            jnp.stack(new_k_p), jnp.stack(new_v_p), jnp.stack(new_re_p), jnp.stack(new_im_p),
            jnp.stack(new_k_s), jnp.stack(new_v_s), jnp.stack(new_re_s), jnp.stack(new_im_s))
```

```python
import functools

import jax
import jax.numpy as jnp
import numpy as np
from jax import lax
from jax.experimental import pallas as pl
from jax.experimental.pallas import tpu as pltpu

F32 = jnp.float32
BF16 = jnp.bfloat16

D_MODEL = 1024
DEPTH = 4
CHUNK = 64
N_HEADS = 16
N_KV_HEADS = 2
HEAD_DIM = 64
GQA_GROUP = N_HEADS // N_KV_HEADS
Q_DIM = N_HEADS * HEAD_DIM
KV_DIM = N_KV_HEADS * HEAD_DIM
QKV_DIM = Q_DIM + 2 * KV_DIM
WINDOW = 128
PAST_LEN = 4096
ROPE_THETA = 10000.0
SSM_GROUP = 16
SSM_GROUPS = D_MODEL // SSM_GROUP
SSM_STATE = 64
D_FF = 2816
DN_ALPHA = (2.0 * DEPTH) ** 0.25
LN_EPS = 1e-5
NEG_INF = -1e30
LOG2E = 1.4426950408889634

LANES = 128
SUBLANES = 8
BF16_SUBLANES = 16
FF_CHUNK = 256
N_FF_CHUNKS = D_FF // FF_CHUNK
SSM_BLOCK = CHUNK
SSM_BLOCK_DIM = SSM_GROUP * SSM_BLOCK
VMEM_LIMIT = 56 * 2 ** 20


def _params(n_axes):
    return pltpu.CompilerParams(dimension_semantics=("arbitrary",) * n_axes, vmem_limit_bytes=VMEM_LIMIT)


def _const_spec(shape):
    zeros = (0,) * len(shape)
    return pl.BlockSpec(shape, lambda *_: zeros, pipeline_mode=pl.Buffered(1))


def _sigmoid(x):
    return 0.5 * jnp.tanh(0.5 * x) + 0.5


def _layer_norm(y, gain, bias):
    mu = jnp.mean(y, axis=-1, keepdims=True)
    d = y - mu
    var = jnp.mean(d * d, axis=-1, keepdims=True)
    return d * lax.rsqrt(var + LN_EPS) * gain + bias


def _ffn_body(x_ref, wup_ref, wdn_ref, g_ref, b_ref, o_ref):
    x = x_ref[...]
    xb = x.astype(BF16)
    acc = DN_ALPHA * x
    for c in range(N_FF_CHUNKS):
        cols = slice(FF_CHUNK * c, FF_CHUNK * (c + 1))
        gate = jnp.dot(xb, wup_ref[:, cols], preferred_element_type=F32)
        up = jnp.dot(xb, wup_ref[:, D_FF + FF_CHUNK * c:D_FF + FF_CHUNK * (c + 1)], preferred_element_type=F32)
        act = gate * _sigmoid(gate) * up
        acc = acc + jnp.dot(act.astype(BF16), wdn_ref[cols, :], preferred_element_type=F32)
    o_ref[...] = _layer_norm(acc, g_ref[...], b_ref[...])


def _ffn_ln(x, wup, wdn, gain, bias, tm):
    n = x.shape[0]
    return pl.pallas_call(
        _ffn_body,
        grid=(n // tm,),
        in_specs=[pl.BlockSpec((tm, D_MODEL), lambda i: (i, 0)),
                  _const_spec((D_MODEL, 2 * D_FF)), _const_spec((D_FF, D_MODEL)),
                  _const_spec((1, D_MODEL)), _const_spec((1, D_MODEL))],
        out_specs=pl.BlockSpec((tm, D_MODEL), lambda i: (i, 0)),
        out_shape=jax.ShapeDtypeStruct((n, D_MODEL), F32),
        compiler_params=_params(1),
        name="ffn_ln",
    )(x, wup, wdn, gain, bias)


def _qkv_body(x_ref, w_ref, b_ref, cos_ref, sa_ref, sb_ref, q_ref, k_ref, v_ref):
    xb = x_ref[...].astype(BF16)
    qkv = jnp.dot(xb, w_ref[...], preferred_element_type=F32) + b_ref[...]
    cos, sin_a, sin_b = cos_ref[...], sa_ref[...], sb_ref[...]
    for s in range((Q_DIM + KV_DIM) // LANES):
        t = qkv[:, s * LANES:(s + 1) * LANES]
        r = t * cos + pltpu.roll(t, LANES - HEAD_DIM // 2, 1) * sin_a + pltpu.roll(t, HEAD_DIM // 2, 1) * sin_b
        if s < Q_DIM // LANES:
            q_ref[:, s * LANES:(s + 1) * LANES] = (r * (HEAD_DIM ** -0.5 * LOG2E)).astype(BF16)
        else:
            k_ref[...] = r
    v_ref[...] = qkv[:, Q_DIM + KV_DIM:]


def _qkv_rope(x, w, b, tabs, tm, tab_tiles):
    n = x.shape[0]
    tab_spec = pl.BlockSpec((tm, LANES), lambda i: (i % tab_tiles, 0))
    return pl.pallas_call(
        _qkv_body,
        grid=(n // tm,),
        in_specs=[pl.BlockSpec((tm, D_MODEL), lambda i: (i, 0)),
                  _const_spec((D_MODEL, QKV_DIM)), _const_spec((1, QKV_DIM)),
                  tab_spec, tab_spec, tab_spec],
        out_specs=[pl.BlockSpec((tm, Q_DIM), lambda i: (i, 0)),
                   pl.BlockSpec((tm, KV_DIM), lambda i: (i, 0)),
                   pl.BlockSpec((tm, KV_DIM), lambda i: (i, 0))],
        out_shape=[jax.ShapeDtypeStruct((n, Q_DIM), BF16),
                   jax.ShapeDtypeStruct((n, KV_DIM), F32),
                   jax.ShapeDtypeStruct((n, KV_DIM), F32)],
        compiler_params=_params(1),
        name="qkv_rope",
    )(x, w, b, *tabs)


def _rope_tables(pos):
    half = HEAD_DIM // 2
    lane = np.arange(LANES)
    inv = ROPE_THETA ** (-(lane % half).astype(np.float64) / half)
    ang = np.asarray(pos, np.float64)[:, None] * inv[None, :]
    first = (lane % HEAD_DIM) < half
    sin = np.sin(ang)
    tabs = (np.cos(ang), np.where(first, -sin, 0.0), np.where(first, 0.0, sin))
    return tuple(jnp.asarray(t.astype(np.float32)) for t in tabs)


PAIR = 2 * CHUNK


def _attn_body(x_ref, q_ref, kp_ref, kc_ref, vp_ref, vc_ref, sink_ref, wot_ref, bo_ref, g_ref, b_ref,
               out_ref, ot_scr, *, tq, has_start):
    span = WINDOW + PAIR
    kcat = jnp.concatenate([kp_ref[...], kc_ref[...]], axis=0)
    vcat = jnp.concatenate([vp_ref[...], vc_ref[...]], axis=0)
    lo_kv = lax.broadcasted_iota(jnp.int32, kcat.shape, 1) < HEAD_DIM
    k_sw = pltpu.roll(kcat, HEAD_DIM, 1)
    k_dup = [jnp.where(lo_kv, kcat, k_sw).astype(BF16), jnp.where(lo_kv, k_sw, kcat).astype(BF16)]
    v_t = vcat.T.astype(BF16)
    ones_rows = jnp.ones((BF16_SUBLANES, v_t.shape[1]), BF16)
    v_aug = [jnp.concatenate([v_t[m * HEAD_DIM:(m + 1) * HEAD_DIM], ones_rows], axis=0) for m in range(N_KV_HEADS)]
    lo_q = lax.broadcasted_iota(jnp.int32, (PAIR, LANES), 1) < HEAD_DIM
    n_cols = GQA_GROUP * PAIR
    tok = lax.broadcasted_iota(jnp.int32, (CHUNK, n_cols), 1) & (PAIR - 1)
    first_chunk = tok < CHUNK
    if has_start:
        after_start = jnp.broadcast_to(pl.program_id(1) > 0, first_chunk.shape)
    slabs_per_kv = GQA_GROUP * HEAD_DIM // LANES

    def scores(pr, m):
        parts = []
        for jj in range(slabs_per_kv):
            col = (slabs_per_kv * m + jj) * LANES
            qs = q_ref[pr * PAIR:(pr + 1) * PAIR, col:col + LANES]
            parts.append(jnp.where(lo_q, qs, jnp.zeros_like(qs)))
            parts.append(jnp.where(lo_q, jnp.zeros_like(qs), qs))
        q_st = jnp.concatenate(parts, axis=0)
        kk = k_dup[m][pr * PAIR:pr * PAIR + span]
        return lax.dot_general(kk, q_st, (((1,), (1,)), ((), ())), preferred_element_type=F32)

    steps = [(pr, m) for pr in range(tq // PAIR) for m in range(N_KV_HEADS)]
    proj_pairs = min(2, tq // PAIR)
    s_next = scores(*steps[0])
    for idx, (pr, m) in enumerate(steps):
        s = s_next
        if idx + 1 < len(steps):
            s_next = scores(*steps[idx + 1])
        rows = slice(pr * PAIR, (pr + 1) * PAIR)
        oldest, older = s[:CHUNK], s[CHUNK:2 * CHUNK]
        if has_start and pr == 0:
            oldest = jnp.where(after_start, oldest, NEG_INF)
            older = jnp.where(after_start, older, NEG_INF)
        edge = jnp.where(first_chunk, oldest, s[3 * CHUNK:])
        s = jnp.concatenate([edge, older, s[2 * CHUNK:3 * CHUNK]], axis=0)
        sink = sink_ref[m] * LOG2E
        mx = jnp.maximum(jnp.max(s, axis=0, keepdims=True), sink)
        e = jnp.exp2(s - mx)
        e_edge = e[:CHUNK]
        p_t = jnp.concatenate([jnp.where(first_chunk, e_edge, 0.0), e[CHUNK:],
                               jnp.where(first_chunk, 0.0, e_edge)], axis=0).astype(BF16)
        o_aug = jnp.dot(v_aug[m][:, pr * PAIR:pr * PAIR + span], p_t, preferred_element_type=F32)
        den = o_aug[HEAD_DIM:HEAD_DIM + 1] + jnp.exp2(sink - mx)
        o_t = o_aug[:HEAD_DIM] * (1.0 / den)
        for hb in range(GQA_GROUP):
            head = GQA_GROUP * m + hb
            ot_scr[head * HEAD_DIM:(head + 1) * HEAD_DIM, rows] = o_t[:, hb * PAIR:(hb + 1) * PAIR].astype(BF16)
        if m == N_KV_HEADS - 1 and (pr + 1) % proj_pairs == 0:
            blk = slice((pr + 1 - proj_pairs) * PAIR, (pr + 1) * PAIR)
            proj_t = jnp.dot(wot_ref[...], ot_scr[:, blk], preferred_element_type=F32)
            y = DN_ALPHA * x_ref[blk, :] + proj_t.T + bo_ref[...]
            out_ref[blk, :] = _layer_norm(y, g_ref[...], b_ref[...])


def _attn_proj_ln(x, q, k_prev, k_cur, v_prev, v_cur, sink_rows, wo_t, bo, gain, bias, *, n_batch, tq, prev_map, has_start):
    n = x.shape[0]
    tiles = n // (n_batch * tq)
    cur = lambda b, i: (b * tiles + i, 0)
    return pl.pallas_call(
        functools.partial(_attn_body, tq=tq, has_start=has_start),
        grid=(n_batch, tiles),
        in_specs=[pl.BlockSpec((tq, D_MODEL), cur), pl.BlockSpec((tq, Q_DIM), cur),
                  pl.BlockSpec((WINDOW, KV_DIM), prev_map), pl.BlockSpec((tq, KV_DIM), cur),
                  pl.BlockSpec((WINDOW, KV_DIM), prev_map), pl.BlockSpec((tq, KV_DIM), cur),
                  _const_spec((N_KV_HEADS, 1, GQA_GROUP * PAIR)),
                  _const_spec((D_MODEL, Q_DIM)), _const_spec((1, D_MODEL)),
                  _const_spec((1, D_MODEL)), _const_spec((1, D_MODEL))],
        out_specs=pl.BlockSpec((tq, D_MODEL), cur),
        out_shape=jax.ShapeDtypeStruct((n, D_MODEL), F32),
        scratch_shapes=[pltpu.VMEM((Q_DIM, tq), BF16)],
        compiler_params=_params(2),
        name="attn_proj_ln",
    )(x, q, k_prev, k_cur, v_prev, v_cur, sink_rows, wo_t, bo, gain, bias)


def _ssm_prep_body(pcol_ref, prow_ref, bcol_ref, brow_ref, ccol_ref, dcol_ref, t_ref, wt_ref, v_ref, are_ref, aim_ref):
    def cmul(x_re, x_im, y_re, y_im):
        return x_re * y_re - x_im * y_im, x_re * y_im + x_im * y_re

    def a_bar(a_re, a_im, dt):
        mag = jnp.exp(dt * a_re)
        ang = dt * a_im
        return mag * jnp.cos(ang), mag * jnp.sin(ang)

    def zoh_factor(l_re, l_im, a_re, a_im):
        den = a_re * a_re + a_im * a_im
        n_re = l_re - 1.0
        return (n_re * a_re + l_im * a_im) / den, (l_im * a_re - n_re * a_im) / den

    n_bits = SSM_BLOCK.bit_length() - 1
    pc = pcol_ref[0]
    ar_c, ai_c, dt_c = pc[:, 0:1], pc[:, 1:2], jnp.exp(pc[:, 2:3])
    lam_c = a_bar(ar_c, ai_c, dt_c)
    fr_c, fi_c = zoh_factor(*lam_c, ar_c, ai_c)
    bc = bcol_ref[0]
    bbr_c = fr_c * bc[:, :SSM_GROUP] - fi_c * bc[:, SSM_GROUP:]
    bbi_c = fr_c * bc[:, SSM_GROUP:] + fi_c * bc[:, :SSM_GROUP]
    pr = prow_ref[0]
    ar_r, ai_r, dt_r = pr[0:1], pr[1:2], jnp.exp(pr[2:3])
    lam_r = a_bar(ar_r, ai_r, dt_r)
    fr_r, fi_r = zoh_factor(*lam_r, ar_r, ai_r)
    br = brow_ref[0]
    bbr_r = fr_r * br[:SSM_GROUP] - fi_r * br[SSM_GROUP:]
    bbi_r = fr_r * br[SSM_GROUP:] + fi_r * br[:SSM_GROUP]
    cc = ccol_ref[0]

    squares = [lam_c]
    for _ in range(n_bits - 1):
        squares.append(cmul(*squares[-1], *squares[-1]))

    def lane_pow(exps):
        acc = None
        for k, (q_re, q_im) in enumerate(squares):
            bit = ((exps >> k) & 1) == 1
            f_re, f_im = jnp.where(bit, q_re, 1.0), jnp.where(bit, q_im, 0.0)
            acc = (f_re, f_im) if acc is None else cmul(*acc, f_re, f_im)
        return acc

    lane = lax.broadcasted_iota(jnp.int32, (SSM_STATE, LANES), 1)
    lo = lane < SSM_BLOCK
    tau = lane & (SSM_BLOCK - 1)
    l0r, l0i = lane_pow(tau)
    l1r, l1i = cmul(l0r, l0i, *lam_c)
    lrr, lri = lane_pow(SSM_BLOCK - 1 - tau)

    g_re, g_im = [], []
    for j in range(SSM_GROUP // 2):
        sl = slice(j * LANES, (j + 1) * LANES)
        cr = jnp.where(lo, cc[:, 2 * j:2 * j + 1], cc[:, 2 * j + 1:2 * j + 2])
        ci = jnp.where(lo, cc[:, SSM_GROUP + 2 * j:SSM_GROUP + 2 * j + 1],
                       cc[:, SSM_GROUP + 2 * j + 1:SSM_GROUP + 2 * j + 2])
        g_re.append(cr * l0r - ci * l0i)
        g_im.append(-(cr * l0i + ci * l0r))
        v_ref[0, :SSM_STATE, sl] = (cr * l1r - ci * l1i).astype(BF16)
        v_ref[0, SSM_STATE:, sl] = (-(cr * l1i + ci * l1r)).astype(BF16)
        b_r = jnp.where(lo, bbr_c[:, 2 * j:2 * j + 1], bbr_c[:, 2 * j + 1:2 * j + 2])
        b_i = jnp.where(lo, bbi_c[:, 2 * j:2 * j + 1], bbi_c[:, 2 * j + 1:2 * j + 2])
        wt_ref[0, :SSM_STATE, sl] = (b_r * lrr - b_i * lri).astype(BF16)
        wt_ref[0, SSM_STATE:, sl] = (b_r * lri + b_i * lrr).astype(BF16)
    g_re = jnp.concatenate(g_re, axis=1)
    g_im = jnp.concatenate(g_im, axis=1)
    kvec = (jnp.dot(bbr_r, g_re, preferred_element_type=F32, precision=lax.Precision.HIGHEST)
            + jnp.dot(bbi_r, g_im, preferred_element_type=F32, precision=lax.Precision.HIGHEST))
    klane = lax.broadcasted_iota(jnp.int32, kvec.shape, 1)
    krow = lax.broadcasted_iota(jnp.int32, kvec.shape, 0)
    kvec = kvec + jnp.where(klane == krow * SSM_BLOCK, dcol_ref[0], 0.0)

    t_in = lax.broadcasted_iota(jnp.int32, (SSM_BLOCK, LANES), 0)
    causal = (lax.broadcasted_iota(jnp.int32, (SSM_BLOCK, LANES), 1) & (SSM_BLOCK - 1)) >= t_in
    for c in range(SSM_GROUP):
        for j in range(SSM_GROUP // 2):
            src = jnp.broadcast_to(kvec[c:c + 1, j * LANES:(j + 1) * LANES], (SSM_BLOCK, LANES))
            shifted = pltpu.roll(src, 0, 1, stride=1, stride_axis=0)
            t_ref[0, c * SSM_BLOCK:(c + 1) * SSM_BLOCK, j * LANES:(j + 1) * LANES] = (
                jnp.where(causal, shifted, 0.0).astype(BF16))
    a_re, a_im = lam_r
    for _ in range(n_bits):
        a_re, a_im = cmul(a_re, a_im, a_re, a_im)
    are_ref[0] = a_re
    aim_ref[0] = a_im


def _ssm_prep(log_dt, a_re, a_im, b_re, b_im, c_re, c_im, d):
    g, p, cg = SSM_GROUPS, SSM_STATE, SSM_GROUP
    params = jnp.stack([a_re, a_im, jnp.broadcast_to(log_dt[:, None], (g, p))], axis=-1)
    pcol, prow = params, params.transpose(0, 2, 1)
    bcol = jnp.concatenate([b_re, b_im], axis=-1)
    brow = bcol.transpose(0, 2, 1)
    ccol = jnp.concatenate([c_re.transpose(0, 2, 1), c_im.transpose(0, 2, 1)], axis=-1)
    dcol = d.reshape(g, cg, 1)
    spec = lambda *shape: pl.BlockSpec((1,) + shape, lambda i: (i, 0, 0))
    n = SSM_BLOCK_DIM
    return pl.pallas_call(
        _ssm_prep_body,
        grid=(g,),
        in_specs=[spec(p, 3), spec(3, p), spec(p, 2 * cg), spec(2 * cg, p), spec(p, 2 * cg), spec(cg, 1)],
        out_specs=[spec(n, n), spec(2 * p, n), spec(2 * p, n), spec(1, p), spec(1, p)],
        out_shape=[jax.ShapeDtypeStruct((g, n, n), BF16)] + [jax.ShapeDtypeStruct((g, 2 * p, n), BF16)] * 2
                  + [jax.ShapeDtypeStruct((g, 1, p), F32)] * 2,
        compiler_params=_params(1),
        name="ssm_prep",
    )(pcol, prow, bcol, brow, ccol, dcol)


def _ssm_in_body(x_ref, w_ref, b_ref, o_ref, *, tm):
    xb = x_ref[...].reshape(tm, D_MODEL).astype(BF16)
    ut = lax.dot_general(w_ref[...], xb, (((1,), (1,)), ((), ())), preferred_element_type=F32)
    o_ref[...] = ut + b_ref[...]


def _ssm_in(x, x_spec, n_tiles, w_t, b_col, tm):
    return pl.pallas_call(
        functools.partial(_ssm_in_body, tm=tm),
        grid=(n_tiles,),
        in_specs=[x_spec, _const_spec((D_MODEL, D_MODEL)), _const_spec((D_MODEL, 1))],
        out_specs=pl.BlockSpec((D_MODEL, tm), lambda i: (0, i)),
        out_shape=jax.ShapeDtypeStruct((D_MODEL, n_tiles * tm), F32),
        compiler_params=_params(1),
        name="ssm_in",
    )(x, w_t, b_col)


def _swap_sublanes_with_slabs(slabs):
    sub = lax.broadcasted_iota(jnp.int32, slabs[0].shape, 1)
    for k in range(3):
        step = 1 << k
        upper = (sub & step) != 0
        nxt = list(slabs)
        for j in range(SUBLANES):
            if j & step:
                continue
            a_lo, a_hi = slabs[j], slabs[j | step]
            nxt[j] = jnp.where(upper, pltpu.roll(a_hi, step, 1), a_lo)
            nxt[j | step] = jnp.where(upper, a_hi, pltpu.roll(a_lo, SUBLANES - step, 1))
        slabs = nxt
    return slabs


def _rows_to_planes(x8, n_rows):
    q_n = n_rows // SUBLANES
    slabs = [jnp.stack([x8[:, (SUBLANES * q + j) * LANES:(SUBLANES * q + j + 1) * LANES] for q in range(q_n)])
             for j in range(SUBLANES)]
    return [p.reshape(n_rows, LANES) for p in _swap_sublanes_with_slabs(slabs)]


def _planes_to_rows(planes, n_rows):
    q_n = n_rows // SUBLANES
    slabs = _swap_sublanes_with_slabs([p.reshape(q_n, SUBLANES, LANES) for p in planes])
    return jnp.concatenate([slabs[j][q] for q in range(q_n) for j in range(SUBLANES)], axis=1)


def _ssm_core_body(x_ref, xs_ref, s0re_ref, s0im_ref, t_ref, wt_ref, v_ref, are_ref, aim_ref,
                   z_ref, zs_ref, sre_ref, sim_ref, pre_ref, pim_ref, s_in, *, half_rows, n_seq):
    n_prompt = 2 * half_rows
    n_rows = s_in.shape[0]
    lo = lax.broadcasted_iota(jnp.int32, (half_rows, LANES), 1) < SSM_BLOCK
    lo_s = lax.broadcasted_iota(jnp.int32, (n_seq, LANES), 1) < SSM_BLOCK
    pad = jnp.zeros((n_rows - n_prompt - n_seq, LANES), F32)
    planes = []
    for h in range(SSM_GROUP // SUBLANES):
        planes += _rows_to_planes(x_ref[SUBLANES * h:SUBLANES * (h + 1), :], half_rows)
    cols = []
    z_planes = [None] * SSM_GROUP
    for j in range(SSM_GROUP // 2):
        xa = planes[2 * j]
        xb = planes[2 * j + 1]
        first = jnp.where(lo, xa, pltpu.roll(xb, SSM_BLOCK, 1))
        second = jnp.where(lo, pltpu.roll(xa, SSM_BLOCK, 1), xb)
        samp = jnp.where(lo_s, xs_ref[2 * j], pltpu.roll(xs_ref[2 * j + 1], SSM_BLOCK, 1))
        cols.append(jnp.concatenate([first, second, samp, pad], axis=0).astype(BF16))
    u = jnp.concatenate(cols, axis=1)

    inj = lax.dot_general(u, wt_ref[0], (((1,), (1,)), ((), ())), preferred_element_type=F32)
    inj_re, inj_im = inj[:, :SSM_STATE], inj[:, SSM_STATE:]
    y_u = jnp.dot(u, t_ref[0], preferred_element_type=F32)
    a_re, a_im = are_ref[0], aim_ref[0]

    s_re = jnp.zeros((n_seq, SSM_STATE), F32)
    s_im = jnp.zeros((n_seq, SSM_STATE), F32)
    for n in range(n_prompt // n_seq):
        r = slice(n * n_seq, (n + 1) * n_seq)
        s_in[r, :SSM_STATE] = s_re
        s_in[r, SSM_STATE:] = s_im
        s_re, s_im = (a_re * s_re - a_im * s_im + inj_re[r], a_re * s_im + a_im * s_re + inj_im[r])
    pre_ref[0] = s_re
    pim_ref[0] = s_im
    rs = slice(n_prompt, n_prompt + n_seq)
    s0_re, s0_im = s0re_ref[0], s0im_ref[0]
    s_in[rs, :SSM_STATE] = s0_re
    s_in[rs, SSM_STATE:] = s0_im
    s_in[n_prompt + n_seq:, :] = jnp.zeros((n_rows - n_prompt - n_seq, 2 * SSM_STATE), F32)
    sre_ref[0] = a_re * s0_re - a_im * s0_im + inj_re[rs]
    sim_ref[0] = a_re * s0_im + a_im * s0_re + inj_im[rs]

    s_bf = s_in[...].astype(BF16)
    width = 2 * LANES
    for h in range(SSM_BLOCK_DIM // width):
        y = y_u[:, h * width:(h + 1) * width] + jnp.dot(s_bf, v_ref[0, :, h * width:(h + 1) * width],
                                                        preferred_element_type=F32)
        z = jax.nn.gelu(y)
        for jj in range(width // LANES):
            j = h * (width // LANES) + jj
            blk = z[:, jj * LANES:(jj + 1) * LANES]
            first, second = blk[:half_rows], blk[half_rows:n_prompt]
            z_planes[2 * j] = jnp.where(lo, first, pltpu.roll(second, SSM_BLOCK, 1))
            z_planes[2 * j + 1] = jnp.where(lo, pltpu.roll(first, SSM_BLOCK, 1), second)
            samp = blk[rs]
            zs_ref[2 * j] = samp
            zs_ref[2 * j + 1] = pltpu.roll(samp, SSM_BLOCK, 1)
    z_ref[...] = jnp.concatenate([_planes_to_rows(z_planes[SUBLANES * h:SUBLANES * (h + 1)], half_rows)
                                  for h in range(SSM_GROUP // SUBLANES)], axis=0).astype(z_ref.dtype)


def _ssm_core(ut, uts3, s0_re, s0_im, ops, *, half_rows, n_seq):
    t_op, wt, v_op, a_re, a_im = ops
    g, p, cg, n = SSM_GROUPS, SSM_STATE, SSM_GROUP, SSM_BLOCK_DIM
    n_rows = 2 * half_rows + 2 * n_seq
    grp = lambda *shape: pl.BlockSpec((1,) + shape, lambda i: (i, 0, 0))
    chan = lambda r: pl.BlockSpec((cg, r, LANES), lambda i: (i, 0, 0))
    rows = pl.BlockSpec((cg, half_rows * LANES), lambda i: (i, 0))
    state = jax.ShapeDtypeStruct((g, n_seq, p), F32)
    return pl.pallas_call(
        functools.partial(_ssm_core_body, half_rows=half_rows, n_seq=n_seq),
        grid=(g,),
        in_specs=[rows, chan(n_seq), grp(n_seq, p), grp(n_seq, p),
                  grp(n, n), grp(2 * p, n), grp(2 * p, n), grp(1, p), grp(1, p)],
        out_specs=[rows, chan(n_seq), grp(n_seq, p), grp(n_seq, p), grp(n_seq, p), grp(n_seq, p)],
        out_shape=[jax.ShapeDtypeStruct(ut.shape, BF16), jax.ShapeDtypeStruct(uts3.shape, F32),
                   state, state, state, state],
        scratch_shapes=[pltpu.VMEM((n_rows, 2 * p), F32)],
        compiler_params=_params(1),
        name="ssm_core",
    )(ut, uts3, s0_re, s0_im, t_op, wt, v_op, a_re, a_im)


def _glu_body(x_ref, zt_ref, w_ref, b_ref, g_ref, bb_ref, o_ref, *, tm):
    chunk = 2 * LANES
    lead = chunk // (tm // x_ref.shape[0])

    def gate_values(h):
        zt = zt_ref[:, h * chunk:(h + 1) * chunk]
        return (jnp.dot(w_ref[:D_MODEL, :], zt, preferred_element_type=F32),
                jnp.dot(w_ref[D_MODEL:, :], zt, preferred_element_type=F32))

    gv_next = gate_values(0)
    for h in range(tm // chunk):
        value, gate = gv_next[0] + b_ref[:D_MODEL, :], gv_next[1] + b_ref[D_MODEL:, :]
        if h + 1 < tm // chunk:
            gv_next = gate_values(h + 1)
        mixed = value * _sigmoid(gate)
        x = x_ref[h * lead:(h + 1) * lead].reshape(chunk, D_MODEL)
        y = DN_ALPHA * x + mixed.T
        o_ref[h * lead:(h + 1) * lead] = _layer_norm(y, g_ref[...], bb_ref[...]).reshape((lead,) + o_ref.shape[1:])


def _glu_ln(x, x_spec, n_tiles, zt, w_t, b_col, gain, bias, tm):
    return pl.pallas_call(
        functools.partial(_glu_body, tm=tm),
        grid=(n_tiles,),
        in_specs=[x_spec, pl.BlockSpec((D_MODEL, tm), lambda i: (0, i)),
                  _const_spec((2 * D_MODEL, D_MODEL)), _const_spec((2 * D_MODEL, 1)),
                  _const_spec((1, D_MODEL)), _const_spec((1, D_MODEL))],
        out_specs=x_spec,
        out_shape=jax.ShapeDtypeStruct(x.shape, F32),
        compiler_params=_params(1),
        name="glu_ln",
    )(x, zt, w_t, b_col, gain, bias)


def kernel(x_prompt, x_sample, cache_k, cache_v, state_ssm_re, state_ssm_im, attn_w_qkv, attn_b_qkv, attn_sinks, attn_w_o, attn_b_o, ssm_w_in, ssm_b_in, ssm_log_dt, ssm_a_re, ssm_a_im, ssm_b_re, ssm_b_im, ssm_c_re, ssm_c_im, ssm_d, ssm_w_glu, ssm_b_glu, ffn_w_up, ffn_w_down, ln_gain, ln_bias):
    n_b, seq, _ = x_prompt.shape
    n_s, seq_s, _ = x_sample.shape
    n_p, n_d = n_b * seq, n_s * seq_s
    tm = 512
    blocks = seq // SSM_BLOCK
    half_rows = (blocks // 2) * n_b
    ssm_tm = 2 * SSM_BLOCK * n_b

    xp = x_prompt.reshape(n_p, D_MODEL)
    xs = x_sample.reshape(n_d, D_MODEL)
    tabs_p = _rope_tables(np.arange(seq))
    tabs_s = _rope_tables(np.tile(PAST_LEN + np.arange(seq_s), n_s))

    ssm_view = (n_b, 2, blocks // 2, SSM_BLOCK, D_MODEL)
    ssm_spec = pl.BlockSpec((n_b, 2, 1, SSM_BLOCK, D_MODEL), lambda i: (0, 0, i, 0, 0))
    flat_spec = pl.BlockSpec((n_d, D_MODEL), lambda i: (i, 0))

    new_k_p, new_v_p, new_re_p, new_im_p = [], [], [], []
    new_k_s, new_v_s, new_re_s, new_im_s = [], [], [], []
    for i in range(DEPTH):
        l = i // 2
        gain = ln_gain[i][:, None, :]
        bias = ln_bias[i][:, None, :]
        if i % 2 == 0:
            w_qkv = attn_w_qkv[l].astype(BF16)
            b_qkv = attn_b_qkv[l][None]
            w_o_t = attn_w_o[l].T.astype(BF16)
            b_o = attn_b_o[l][None]
            sink_rows = jnp.repeat(attn_sinks[l].astype(F32), PAIR).reshape(N_KV_HEADS, 1, GQA_GROUP * PAIR)
            qp, kp, vp = _qkv_rope(xp, w_qkv, b_qkv, tabs_p, tm, seq // tm)
            qs, ks, vs = _qkv_rope(xs, w_qkv, b_qkv, tabs_s, n_d, 1)
            prev_p = lambda b, t: (jnp.maximum(b * (seq // WINDOW) + t * (tm // WINDOW) - 1, 0), 0)
            xp = _attn_proj_ln(xp, qp, kp, kp, vp, vp, sink_rows, w_o_t, b_o, gain[0], bias[0],
                               n_batch=n_b, tq=tm, prev_map=prev_p, has_start=True)
            pad_pair = lambda a: jnp.pad(a.reshape(n_s, seq_s, -1), ((0, 0), (0, PAIR - seq_s), (0, 0))).reshape(n_s * PAIR, -1)
            ck = cache_k[l].reshape(n_s * WINDOW, KV_DIM)
            cv = cache_v[l].reshape(n_s * WINDOW, KV_DIM)
            xs = _attn_proj_ln(pad_pair(xs), pad_pair(qs), ck, pad_pair(ks), cv, pad_pair(vs), sink_rows, w_o_t, b_o,
                               gain[0], bias[0], n_batch=n_s, tq=PAIR, prev_map=lambda b, t: (b, 0), has_start=False)
            xs = xs.reshape(n_s, PAIR, D_MODEL)[:, :seq_s].reshape(n_d, D_MODEL)
            last = lambda a: a.reshape(n_b, seq, KV_DIM)[:, seq - WINDOW:].reshape(n_b, WINDOW, N_KV_HEADS, HEAD_DIM)
            new_k_p.append(last(kp))
            new_v_p.append(last(vp))
            ks4 = ks.reshape(n_s, seq_s, N_KV_HEADS, HEAD_DIM)
            vs4 = vs.reshape(n_s, seq_s, N_KV_HEADS, HEAD_DIM)
            new_k_s.append(jnp.concatenate([cache_k[l], ks4], axis=1)[:, -WINDOW:])
            new_v_s.append(jnp.concatenate([cache_v[l], vs4], axis=1)[:, -WINDOW:])
        else:
            w_in_t = ssm_w_in[l].T.astype(BF16)
            b_in = ssm_b_in[l][:, None]
            w_glu_t = ssm_w_glu[l].T.astype(BF16)
            b_glu = ssm_b_glu[l][:, None]
            ops = _ssm_prep(ssm_log_dt[l], ssm_a_re[l], ssm_a_im[l], ssm_b_re[l], ssm_b_im[l],
                            ssm_c_re[l], ssm_c_im[l], ssm_d[l])
            ut = _ssm_in(xp.reshape(ssm_view), ssm_spec, blocks // 2, w_in_t, b_in, ssm_tm)
            uts = _ssm_in(xs, flat_spec, 1, w_in_t, b_in, n_d)
            uts3 = jnp.pad(uts.reshape(D_MODEL, n_s, seq_s), ((0, 0), (0, 0), (0, LANES - seq_s)))
            s0_re = state_ssm_re[l].transpose(1, 0, 2)
            s0_im = state_ssm_im[l].transpose(1, 0, 2)
            zt, zs3, s_re, s_im, p_re, p_im = _ssm_core(ut, uts3, s0_re, s0_im, ops, half_rows=half_rows, n_seq=n_s)
            zst = zs3[:, :, :seq_s].reshape(D_MODEL, n_d).astype(BF16)
            xp = _glu_ln(xp.reshape(ssm_view), ssm_spec, blocks // 2, zt, w_glu_t, b_glu, gain[0], bias[0],
                         ssm_tm).reshape(n_p, D_MODEL)
            xs = _glu_ln(xs, flat_spec, 1, zst, w_glu_t, b_glu, gain[0], bias[0], n_d)
            new_re_p.append(p_re.transpose(1, 0, 2))
            new_im_p.append(p_im.transpose(1, 0, 2))
            new_re_s.append(s_re.transpose(1, 0, 2))
            new_im_s.append(s_im.transpose(1, 0, 2))
        w_up = ffn_w_up[i].astype(BF16)
        w_dn = ffn_w_down[i].astype(BF16)
        xp = _ffn_ln(xp, w_up, w_dn, gain[1], bias[1], 2 * tm)
        xs = _ffn_ln(xs, w_up, w_dn, gain[1], bias[1], n_d)
    return (xp.reshape(x_prompt.shape), xs.reshape(x_sample.shape),
            jnp.stack(new_k_p), jnp.stack(new_v_p), jnp.stack(new_re_p), jnp.stack(new_im_p),
            jnp.stack(new_k_s), jnp.stack(new_v_s), jnp.stack(new_re_s), jnp.stack(new_im_s))
```

```python
import functools

import jax
import jax.numpy as jnp
import numpy as np
from jax import lax
from jax.experimental import pallas as pl
from jax.experimental.pallas import tpu as pltpu

F32 = jnp.float32
BF16 = jnp.bfloat16

D_MODEL = 1024
DEPTH = 4
CHUNK = 64
N_HEADS = 16
N_KV_HEADS = 2
HEAD_DIM = 64
GQA_GROUP = N_HEADS // N_KV_HEADS
Q_DIM = N_HEADS * HEAD_DIM
KV_DIM = N_KV_HEADS * HEAD_DIM
QKV_DIM = Q_DIM + 2 * KV_DIM
WINDOW = 128
PAST_LEN = 4096
ROPE_THETA = 10000.0
SSM_GROUP = 16
SSM_GROUPS = D_MODEL // SSM_GROUP
SSM_STATE = 64
D_FF = 2816
DN_ALPHA = (2.0 * DEPTH) ** 0.25
LN_EPS = 1e-5
NEG_INF = -1e30
LOG2E = 1.4426950408889634

LANES = 128
SUBLANES = 8
BF16_SUBLANES = 16
FF_CHUNK = 256
N_FF_CHUNKS = D_FF // FF_CHUNK
SSM_BLOCK = CHUNK
SSM_BLOCK_DIM = SSM_GROUP * SSM_BLOCK
VMEM_LIMIT = 56 * 2 ** 20


def _params(n_axes):
    return pltpu.CompilerParams(dimension_semantics=("arbitrary",) * n_axes, vmem_limit_bytes=VMEM_LIMIT)


def _const_spec(shape):
    zeros = (0,) * len(shape)
    return pl.BlockSpec(shape, lambda *_: zeros, pipeline_mode=pl.Buffered(1))


def _sigmoid(x):
    return 0.5 * jnp.tanh(0.5 * x) + 0.5


def _gelu_tanh(x):
    k = (2.0 / np.pi) ** 0.5
    half_x = 0.5 * x
    return half_x * jnp.tanh(x * (k + (k * 0.044715) * (x * x))) + half_x


def _layer_norm(y, gain, bias):
    mu = jnp.mean(y, axis=-1, keepdims=True)
    d = y - mu
    var = jnp.mean(d * d, axis=-1, keepdims=True)
    return d * lax.rsqrt(var + LN_EPS) * gain + bias


def _ffn_body(x_ref, wup_ref, wdn_ref, g_ref, b_ref, o_ref):
    x = x_ref[...]
    xb = x.astype(BF16)
    acc = DN_ALPHA * x
    for c in range(N_FF_CHUNKS):
        cols = slice(FF_CHUNK * c, FF_CHUNK * (c + 1))
        gate = jnp.dot(xb, wup_ref[:, cols], preferred_element_type=F32)
        up = jnp.dot(xb, wup_ref[:, D_FF + FF_CHUNK * c:D_FF + FF_CHUNK * (c + 1)], preferred_element_type=F32)
        act = gate * _sigmoid(gate) * up
        acc = acc + jnp.dot(act.astype(BF16), wdn_ref[cols, :], preferred_element_type=F32)
    o_ref[...] = _layer_norm(acc, g_ref[...], b_ref[...])


def _ffn_ln(x, wup, wdn, gain, bias, tm):
    n = x.shape[0]
    return pl.pallas_call(
        _ffn_body,
        grid=(n // tm,),
        in_specs=[pl.BlockSpec((tm, D_MODEL), lambda i: (i, 0)),
                  _const_spec((D_MODEL, 2 * D_FF)), _const_spec((D_FF, D_MODEL)),
                  _const_spec((1, D_MODEL)), _const_spec((1, D_MODEL))],
        out_specs=pl.BlockSpec((tm, D_MODEL), lambda i: (i, 0)),
        out_shape=jax.ShapeDtypeStruct((n, D_MODEL), F32),
        compiler_params=_params(1),
        name="ffn_ln",
    )(x, wup, wdn, gain, bias)


def _qkv_body(x_ref, w_ref, b_ref, cos_ref, sin_ref, q_ref, k_ref, v_ref):
    half = HEAD_DIM // 2
    chunk = 2 * LANES
    n_chunks = x_ref.shape[0] // chunk
    nt = (((1,), (1,)), ((), ()))

    def project(h):
        xb = x_ref[h * chunk:(h + 1) * chunk, :].astype(BF16)
        return (lax.dot_general(w_ref[:Q_DIM, :], xb, nt, preferred_element_type=F32),
                lax.dot_general(w_ref[Q_DIM:, :], xb, nt, preferred_element_type=F32))

    def rope(t, cos, sin):
        rot = t.reshape(t.shape[0] // HEAD_DIM, 2, half, chunk)
        first, second = rot[:, 0], rot[:, 1]
        return jnp.stack([first * cos - second * sin, second * cos + first * sin], axis=1).reshape(t.shape)

    nxt = project(0)
    for h in range(n_chunks):
        cols = slice(h * chunk, (h + 1) * chunk)
        q_t, kv_t = nxt[0] + b_ref[:Q_DIM, :], nxt[1] + b_ref[Q_DIM:, :]
        if h + 1 < n_chunks:
            nxt = project(h + 1)
        cos, sin = cos_ref[:, cols], sin_ref[:, cols]
        q_ref[:, cols] = (rope(q_t, cos, sin) * (HEAD_DIM ** -0.5 * LOG2E)).astype(BF16)
        k_ref[:, cols] = rope(kv_t[:KV_DIM], cos, sin)
        v_ref[:, cols] = kv_t[KV_DIM:]


def _qkv_rope(x, w_t, b_col, tabs, tm, tab_tiles):
    n = x.shape[0]
    tab_spec = pl.BlockSpec((HEAD_DIM // 2, tm), lambda i: (0, i % tab_tiles))
    col = lambda rows: pl.BlockSpec((rows, tm), lambda i: (0, i))
    return pl.pallas_call(
        _qkv_body,
        grid=(n // tm,),
        in_specs=[pl.BlockSpec((tm, D_MODEL), lambda i: (i, 0)),
                  _const_spec((QKV_DIM, D_MODEL)), _const_spec((QKV_DIM, 1)), tab_spec, tab_spec],
        out_specs=[col(Q_DIM), col(KV_DIM), col(KV_DIM)],
        out_shape=[jax.ShapeDtypeStruct((Q_DIM, n), BF16),
                   jax.ShapeDtypeStruct((KV_DIM, n), F32),
                   jax.ShapeDtypeStruct((KV_DIM, n), F32)],
        compiler_params=_params(1),
        name="qkv_rope",
    )(x, w_t, b_col, *tabs)


def _rope_tables(pos):
    half = HEAD_DIM // 2
    inv = ROPE_THETA ** (-np.arange(half, dtype=np.float64) / half)
    ang = inv[:, None] * np.asarray(pos, np.float64)[None, :]
    return tuple(jnp.asarray(t.astype(np.float32)) for t in (np.cos(ang), np.sin(ang)))


PAIR = 2 * CHUNK


def _attn_body(x_ref, q_ref, kp_ref, kc_ref, vp_ref, vc_ref, sink_ref, wot_ref, bo_ref, g_ref, b_ref,
               out_ref, ot_scr, *, tq, has_start):
    span = WINDOW + PAIR
    k_t = jnp.concatenate([kp_ref[...], kc_ref[...]], axis=1)
    v_t = jnp.concatenate([vp_ref[...], vc_ref[...]], axis=1).astype(BF16)
    k_rows = [k_t[m * HEAD_DIM:(m + 1) * HEAD_DIM].T.astype(BF16) for m in range(N_KV_HEADS)]
    ones_rows = jnp.ones((BF16_SUBLANES, v_t.shape[1]), BF16)
    v_aug = [jnp.concatenate([v_t[m * HEAD_DIM:(m + 1) * HEAD_DIM], ones_rows], axis=0) for m in range(N_KV_HEADS)]
    n_cols = GQA_GROUP * PAIR
    tok = lax.broadcasted_iota(jnp.int32, (CHUNK, n_cols), 1) & (PAIR - 1)
    first_chunk = tok < CHUNK
    if has_start:
        after_start = jnp.broadcast_to(pl.program_id(1) > 0, first_chunk.shape)

    def scores(pr, m):
        heads = range(GQA_GROUP * m, GQA_GROUP * (m + 1))
        q_m = jnp.concatenate([q_ref[h * HEAD_DIM:(h + 1) * HEAD_DIM, pr * PAIR:(pr + 1) * PAIR] for h in heads], axis=1)
        return jnp.dot(k_rows[m][pr * PAIR:pr * PAIR + span], q_m, preferred_element_type=F32)

    steps = [(pr, m) for pr in range(tq // PAIR) for m in range(N_KV_HEADS)]
    proj_pairs = min(2, tq // PAIR)
    s_next = scores(*steps[0])
    for idx, (pr, m) in enumerate(steps):
        s = s_next
        if idx + 1 < len(steps):
            s_next = scores(*steps[idx + 1])
        rows = slice(pr * PAIR, (pr + 1) * PAIR)
        oldest, older = s[:CHUNK], s[CHUNK:2 * CHUNK]
        if has_start and pr == 0:
            oldest = jnp.where(after_start, oldest, NEG_INF)
            older = jnp.where(after_start, older, NEG_INF)
        edge = jnp.where(first_chunk, oldest, s[3 * CHUNK:])
        s = jnp.concatenate([edge, older, s[2 * CHUNK:3 * CHUNK]], axis=0)
        sink = sink_ref[m] * LOG2E
        mx = jnp.maximum(jnp.max(s, axis=0, keepdims=True), sink)
        e = jnp.exp2(s - mx)
        e_edge = e[:CHUNK]
        p_t = jnp.concatenate([jnp.where(first_chunk, e_edge, 0.0), e[CHUNK:],
                               jnp.where(first_chunk, 0.0, e_edge)], axis=0).astype(BF16)
        o_aug = jnp.dot(v_aug[m][:, pr * PAIR:pr * PAIR + span], p_t, preferred_element_type=F32)
        den = o_aug[HEAD_DIM:HEAD_DIM + 1] + jnp.exp2(sink - mx)
        o_t = o_aug[:HEAD_DIM] * (1.0 / den)
        for hb in range(GQA_GROUP):
            head = GQA_GROUP * m + hb
            ot_scr[head * HEAD_DIM:(head + 1) * HEAD_DIM, rows] = o_t[:, hb * PAIR:(hb + 1) * PAIR].astype(BF16)
        if m == N_KV_HEADS - 1 and (pr + 1) % proj_pairs == 0:
            blk = slice((pr + 1 - proj_pairs) * PAIR, (pr + 1) * PAIR)
            proj_t = jnp.dot(wot_ref[...], ot_scr[:, blk], preferred_element_type=F32)
            y = DN_ALPHA * x_ref[blk, :] + proj_t.T + bo_ref[...]
            out_ref[blk, :] = _layer_norm(y, g_ref[...], b_ref[...])


def _attn_proj_ln(x, q_t, k_prev, k_cur, v_prev, v_cur, sink_rows, wo_t, bo, gain, bias, *, n_batch, tq, prev_map, has_start):
    n = x.shape[0]
    tiles = n // (n_batch * tq)
    cur = lambda b, i: (b * tiles + i, 0)
    cur_t = lambda b, i: (0, b * tiles + i)
    return pl.pallas_call(
        functools.partial(_attn_body, tq=tq, has_start=has_start),
        grid=(n_batch, tiles),
        in_specs=[pl.BlockSpec((tq, D_MODEL), cur), pl.BlockSpec((Q_DIM, tq), cur_t),
                  pl.BlockSpec((KV_DIM, WINDOW), prev_map), pl.BlockSpec((KV_DIM, tq), cur_t),
                  pl.BlockSpec((KV_DIM, WINDOW), prev_map), pl.BlockSpec((KV_DIM, tq), cur_t),
                  _const_spec((N_KV_HEADS, 1, GQA_GROUP * PAIR)),
                  _const_spec((D_MODEL, Q_DIM)), _const_spec((1, D_MODEL)),
                  _const_spec((1, D_MODEL)), _const_spec((1, D_MODEL))],
        out_specs=pl.BlockSpec((tq, D_MODEL), cur),
        out_shape=jax.ShapeDtypeStruct((n, D_MODEL), F32),
        scratch_shapes=[pltpu.VMEM((Q_DIM, tq), BF16)],
        compiler_params=_params(2),
        name="attn_proj_ln",
    )(x, q_t, k_prev, k_cur, v_prev, v_cur, sink_rows, wo_t, bo, gain, bias)


def _ssm_prep_body(pcol_ref, prow_ref, bcol_ref, brow_ref, ccol_ref, dcol_ref, t_ref, wt_ref, v_ref, are_ref, aim_ref):
    def cmul(x_re, x_im, y_re, y_im):
        return x_re * y_re - x_im * y_im, x_re * y_im + x_im * y_re

    def a_bar(a_re, a_im, dt):
        mag = jnp.exp(dt * a_re)
        ang = dt * a_im
        return mag * jnp.cos(ang), mag * jnp.sin(ang)

    def zoh_factor(l_re, l_im, a_re, a_im):
        den = a_re * a_re + a_im * a_im
        n_re = l_re - 1.0
        return (n_re * a_re + l_im * a_im) / den, (l_im * a_re - n_re * a_im) / den

    n_bits = SSM_BLOCK.bit_length() - 1
    pc = pcol_ref[0]
    ar_c, ai_c, dt_c = pc[:, 0:1], pc[:, 1:2], jnp.exp(pc[:, 2:3])
    lam_c = a_bar(ar_c, ai_c, dt_c)
    fr_c, fi_c = zoh_factor(*lam_c, ar_c, ai_c)
    bc = bcol_ref[0]
    bbr_c = fr_c * bc[:, :SSM_GROUP] - fi_c * bc[:, SSM_GROUP:]
    bbi_c = fr_c * bc[:, SSM_GROUP:] + fi_c * bc[:, :SSM_GROUP]
    pr = prow_ref[0]
    ar_r, ai_r, dt_r = pr[0:1], pr[1:2], jnp.exp(pr[2:3])
    lam_r = a_bar(ar_r, ai_r, dt_r)
    fr_r, fi_r = zoh_factor(*lam_r, ar_r, ai_r)
    br = brow_ref[0]
    bbr_r = fr_r * br[:SSM_GROUP] - fi_r * br[SSM_GROUP:]
    bbi_r = fr_r * br[SSM_GROUP:] + fi_r * br[:SSM_GROUP]
    cc = ccol_ref[0]

    squares = [lam_c]
    for _ in range(n_bits - 1):
        squares.append(cmul(*squares[-1], *squares[-1]))

    def lane_pow(exps):
        acc = None
        for k, (q_re, q_im) in enumerate(squares):
            bit = ((exps >> k) & 1) == 1
            f_re, f_im = jnp.where(bit, q_re, 1.0), jnp.where(bit, q_im, 0.0)
            acc = (f_re, f_im) if acc is None else cmul(*acc, f_re, f_im)
        return acc

    lane = lax.broadcasted_iota(jnp.int32, (SSM_STATE, LANES), 1)
    lo = lane < SSM_BLOCK
    tau = lane & (SSM_BLOCK - 1)
    l0r, l0i = lane_pow(tau)
    l1r, l1i = cmul(l0r, l0i, *lam_c)
    lrr, lri = lane_pow(SSM_BLOCK - 1 - tau)

    g_re, g_im = [], []
    for j in range(SSM_GROUP // 2):
        sl = slice(j * LANES, (j + 1) * LANES)
        cr = jnp.where(lo, cc[:, 2 * j:2 * j + 1], cc[:, 2 * j + 1:2 * j + 2])
        ci = jnp.where(lo, cc[:, SSM_GROUP + 2 * j:SSM_GROUP + 2 * j + 1],
                       cc[:, SSM_GROUP + 2 * j + 1:SSM_GROUP + 2 * j + 2])
        g_re.append(cr * l0r - ci * l0i)
        g_im.append(-(cr * l0i + ci * l0r))
        v_ref[0, :SSM_STATE, sl] = (cr * l1r - ci * l1i).astype(BF16)
        v_ref[0, SSM_STATE:, sl] = (-(cr * l1i + ci * l1r)).astype(BF16)
        b_r = jnp.where(lo, bbr_c[:, 2 * j:2 * j + 1], bbr_c[:, 2 * j + 1:2 * j + 2])
        b_i = jnp.where(lo, bbi_c[:, 2 * j:2 * j + 1], bbi_c[:, 2 * j + 1:2 * j + 2])
        wt_ref[0, :SSM_STATE, sl] = (b_r * lrr - b_i * lri).astype(BF16)
        wt_ref[0, SSM_STATE:, sl] = (b_r * lri + b_i * lrr).astype(BF16)
    g_re = jnp.concatenate(g_re, axis=1)
    g_im = jnp.concatenate(g_im, axis=1)
    kvec = (jnp.dot(bbr_r, g_re, preferred_element_type=F32, precision=lax.Precision.HIGHEST)
            + jnp.dot(bbi_r, g_im, preferred_element_type=F32, precision=lax.Precision.HIGHEST))
    klane = lax.broadcasted_iota(jnp.int32, kvec.shape, 1)
    krow = lax.broadcasted_iota(jnp.int32, kvec.shape, 0)
    kvec = kvec + jnp.where(klane == krow * SSM_BLOCK, dcol_ref[0], 0.0)

    t_in = lax.broadcasted_iota(jnp.int32, (SSM_BLOCK, LANES), 0)
    causal = (lax.broadcasted_iota(jnp.int32, (SSM_BLOCK, LANES), 1) & (SSM_BLOCK - 1)) >= t_in
    for c in range(SSM_GROUP):
        for j in range(SSM_GROUP // 2):
            src = jnp.broadcast_to(kvec[c:c + 1, j * LANES:(j + 1) * LANES], (SSM_BLOCK, LANES))
            shifted = pltpu.roll(src, 0, 1, stride=1, stride_axis=0)
            t_ref[0, c * SSM_BLOCK:(c + 1) * SSM_BLOCK, j * LANES:(j + 1) * LANES] = (
                jnp.where(causal, shifted, 0.0).astype(BF16))
    a_re, a_im = lam_r
    for _ in range(n_bits):
        a_re, a_im = cmul(a_re, a_im, a_re, a_im)
    are_ref[0] = a_re
    aim_ref[0] = a_im


def _ssm_prep(log_dt, a_re, a_im, b_re, b_im, c_re, c_im, d):
    g, p, cg = SSM_GROUPS, SSM_STATE, SSM_GROUP
    params = jnp.stack([a_re, a_im, jnp.broadcast_to(log_dt[:, None], (g, p))], axis=-1)
    pcol, prow = params, params.transpose(0, 2, 1)
    bcol = jnp.concatenate([b_re, b_im], axis=-1)
    brow = bcol.transpose(0, 2, 1)
    ccol = jnp.concatenate([c_re.transpose(0, 2, 1), c_im.transpose(0, 2, 1)], axis=-1)
    dcol = d.reshape(g, cg, 1)
    spec = lambda *shape: pl.BlockSpec((1,) + shape, lambda i: (i, 0, 0))
    n = SSM_BLOCK_DIM
    return pl.pallas_call(
        _ssm_prep_body,
        grid=(g,),
        in_specs=[spec(p, 3), spec(3, p), spec(p, 2 * cg), spec(2 * cg, p), spec(p, 2 * cg), spec(cg, 1)],
        out_specs=[spec(n, n), spec(2 * p, n), spec(2 * p, n), spec(1, p), spec(1, p)],
        out_shape=[jax.ShapeDtypeStruct((g, n, n), BF16)] + [jax.ShapeDtypeStruct((g, 2 * p, n), BF16)] * 2
                  + [jax.ShapeDtypeStruct((g, 1, p), F32)] * 2,
        compiler_params=_params(1),
        name="ssm_prep",
    )(pcol, prow, bcol, brow, ccol, dcol)


def _ssm_in_body(x_ref, w_ref, b_ref, o_ref, *, tm):
    xb = x_ref[...].reshape(tm, D_MODEL).astype(BF16)
    ut = lax.dot_general(w_ref[...], xb, (((1,), (1,)), ((), ())), preferred_element_type=F32)
    o_ref[...] = ut + b_ref[...]


def _ssm_in(x, x_spec, n_tiles, w_t, b_col, tm):
    return pl.pallas_call(
        functools.partial(_ssm_in_body, tm=tm),
        grid=(n_tiles,),
        in_specs=[x_spec, _const_spec((D_MODEL, D_MODEL)), _const_spec((D_MODEL, 1))],
        out_specs=pl.BlockSpec((D_MODEL, tm), lambda i: (0, i)),
        out_shape=jax.ShapeDtypeStruct((D_MODEL, n_tiles * tm), F32),
        compiler_params=_params(1),
        name="ssm_in",
    )(x, w_t, b_col)


def _swap_sublanes_with_slabs(slabs):
    sub = lax.broadcasted_iota(jnp.int32, slabs[0].shape, 1)
    for k in range(3):
        step = 1 << k
        upper = (sub & step) != 0
        nxt = list(slabs)
        for j in range(SUBLANES):
            if j & step:
                continue
            a_lo, a_hi = slabs[j], slabs[j | step]
            nxt[j] = jnp.where(upper, pltpu.roll(a_hi, step, 1), a_lo)
            nxt[j | step] = jnp.where(upper, a_hi, pltpu.roll(a_lo, SUBLANES - step, 1))
        slabs = nxt
    return slabs


def _rows_to_planes(x8, n_rows):
    q_n = n_rows // SUBLANES
    slabs = [jnp.stack([x8[:, (SUBLANES * q + j) * LANES:(SUBLANES * q + j + 1) * LANES] for q in range(q_n)])
             for j in range(SUBLANES)]
    return [p.reshape(n_rows, LANES) for p in _swap_sublanes_with_slabs(slabs)]


def _planes_to_rows(planes, n_rows):
    q_n = n_rows // SUBLANES
    slabs = _swap_sublanes_with_slabs([p.reshape(q_n, SUBLANES, LANES) for p in planes])
    return jnp.concatenate([slabs[j][q] for q in range(q_n) for j in range(SUBLANES)], axis=1)


def _ssm_core_body(x_ref, xs_ref, s0re_ref, s0im_ref, t_ref, wt_ref, v_ref, are_ref, aim_ref,
                   z_ref, zs_ref, sre_ref, sim_ref, pre_ref, pim_ref, s_in, *, half_rows, n_seq):
    n_prompt = 2 * half_rows
    n_rows = s_in.shape[0]
    lo = lax.broadcasted_iota(jnp.int32, (half_rows, LANES), 1) < SSM_BLOCK
    lo_s = lax.broadcasted_iota(jnp.int32, (n_seq, LANES), 1) < SSM_BLOCK
    pad = jnp.zeros((n_rows - n_prompt - n_seq, LANES), F32)
    planes = []
    for h in range(SSM_GROUP // SUBLANES):
        planes += _rows_to_planes(x_ref[SUBLANES * h:SUBLANES * (h + 1), :], half_rows)
    cols = []
    z_planes = [None] * SSM_GROUP
    for j in range(SSM_GROUP // 2):
        xa = planes[2 * j]
        xb = planes[2 * j + 1]
        first = jnp.where(lo, xa, pltpu.roll(xb, SSM_BLOCK, 1))
        second = jnp.where(lo, pltpu.roll(xa, SSM_BLOCK, 1), xb)
        samp = jnp.where(lo_s, xs_ref[2 * j], pltpu.roll(xs_ref[2 * j + 1], SSM_BLOCK, 1))
        cols.append(jnp.concatenate([first, second, samp, pad], axis=0).astype(BF16))
    u = jnp.concatenate(cols, axis=1)

    inj = lax.dot_general(u, wt_ref[0], (((1,), (1,)), ((), ())), preferred_element_type=F32)
    inj_re, inj_im = inj[:, :SSM_STATE], inj[:, SSM_STATE:]
    y_u = jnp.dot(u, t_ref[0], preferred_element_type=F32)
    a_re, a_im = are_ref[0], aim_ref[0]

    s_re = jnp.zeros((n_seq, SSM_STATE), F32)
    s_im = jnp.zeros((n_seq, SSM_STATE), F32)
    for n in range(n_prompt // n_seq):
        r = slice(n * n_seq, (n + 1) * n_seq)
        s_in[r, :SSM_STATE] = s_re
        s_in[r, SSM_STATE:] = s_im
        s_re, s_im = (a_re * s_re - a_im * s_im + inj_re[r], a_re * s_im + a_im * s_re + inj_im[r])
    pre_ref[0] = s_re
    pim_ref[0] = s_im
    rs = slice(n_prompt, n_prompt + n_seq)
    s0_re, s0_im = s0re_ref[0], s0im_ref[0]
    s_in[rs, :SSM_STATE] = s0_re
    s_in[rs, SSM_STATE:] = s0_im
    s_in[n_prompt + n_seq:, :] = jnp.zeros((n_rows - n_prompt - n_seq, 2 * SSM_STATE), F32)
    sre_ref[0] = a_re * s0_re - a_im * s0_im + inj_re[rs]
    sim_ref[0] = a_re * s0_im + a_im * s0_re + inj_im[rs]

    s_bf = s_in[...].astype(BF16)
    width = 2 * LANES
    for h in range(SSM_BLOCK_DIM // width):
        y = y_u[:, h * width:(h + 1) * width] + jnp.dot(s_bf, v_ref[0, :, h * width:(h + 1) * width],
                                                        preferred_element_type=F32)
        z = _gelu_tanh(y)
        for jj in range(width // LANES):
            j = h * (width // LANES) + jj
            blk = z[:, jj * LANES:(jj + 1) * LANES]
            first, second = blk[:half_rows], blk[half_rows:n_prompt]
            z_planes[2 * j] = jnp.where(lo, first, pltpu.roll(second, SSM_BLOCK, 1))
            z_planes[2 * j + 1] = jnp.where(lo, pltpu.roll(first, SSM_BLOCK, 1), second)
            samp = blk[rs]
            zs_ref[2 * j] = samp
            zs_ref[2 * j + 1] = pltpu.roll(samp, SSM_BLOCK, 1)
    z_ref[...] = jnp.concatenate([_planes_to_rows(z_planes[SUBLANES * h:SUBLANES * (h + 1)], half_rows)
                                  for h in range(SSM_GROUP // SUBLANES)], axis=0).astype(z_ref.dtype)


def _ssm_core(ut, uts3, s0_re, s0_im, ops, *, half_rows, n_seq):
    t_op, wt, v_op, a_re, a_im = ops
    g, p, cg, n = SSM_GROUPS, SSM_STATE, SSM_GROUP, SSM_BLOCK_DIM
    n_rows = 2 * half_rows + 2 * n_seq
    grp = lambda *shape: pl.BlockSpec((1,) + shape, lambda i: (i, 0, 0))
    chan = lambda r: pl.BlockSpec((cg, r, LANES), lambda i: (i, 0, 0))
    rows = pl.BlockSpec((cg, half_rows * LANES), lambda i: (i, 0))
    state = jax.ShapeDtypeStruct((g, n_seq, p), F32)
    return pl.pallas_call(
        functools.partial(_ssm_core_body, half_rows=half_rows, n_seq=n_seq),
        grid=(g,),
        in_specs=[rows, chan(n_seq), grp(n_seq, p), grp(n_seq, p),
                  grp(n, n), grp(2 * p, n), grp(2 * p, n), grp(1, p), grp(1, p)],
        out_specs=[rows, chan(n_seq), grp(n_seq, p), grp(n_seq, p), grp(n_seq, p), grp(n_seq, p)],
        out_shape=[jax.ShapeDtypeStruct(ut.shape, BF16), jax.ShapeDtypeStruct(uts3.shape, F32),
                   state, state, state, state],
        scratch_shapes=[pltpu.VMEM((n_rows, 2 * p), F32)],
        compiler_params=_params(1),
        name="ssm_core",
    )(ut, uts3, s0_re, s0_im, t_op, wt, v_op, a_re, a_im)


def _glu_body(x_ref, zt_ref, w_ref, b_ref, g_ref, bb_ref, o_ref, *, tm):
    chunk = 2 * LANES
    lead = chunk // (tm // x_ref.shape[0])

    def gate_values(h):
        zt = zt_ref[:, h * chunk:(h + 1) * chunk]
        return (jnp.dot(w_ref[:D_MODEL, :], zt, preferred_element_type=F32),
                jnp.dot(w_ref[D_MODEL:, :], zt, preferred_element_type=F32))

    gv_next = gate_values(0)
    for h in range(tm // chunk):
        value, gate = gv_next[0] + b_ref[:D_MODEL, :], gv_next[1] + b_ref[D_MODEL:, :]
        if h + 1 < tm // chunk:
            gv_next = gate_values(h + 1)
        mixed = value * _sigmoid(gate)
        x = x_ref[h * lead:(h + 1) * lead].reshape(chunk, D_MODEL)
        y = DN_ALPHA * x + mixed.T
        o_ref[h * lead:(h + 1) * lead] = _layer_norm(y, g_ref[...], bb_ref[...]).reshape((lead,) + o_ref.shape[1:])


def _glu_ln(x, x_spec, n_tiles, zt, w_t, b_col, gain, bias, tm):
    return pl.pallas_call(
        functools.partial(_glu_body, tm=tm),
        grid=(n_tiles,),
        in_specs=[x_spec, pl.BlockSpec((D_MODEL, tm), lambda i: (0, i)),
                  _const_spec((2 * D_MODEL, D_MODEL)), _const_spec((2 * D_MODEL, 1)),
                  _const_spec((1, D_MODEL)), _const_spec((1, D_MODEL))],
        out_specs=x_spec,
        out_shape=jax.ShapeDtypeStruct(x.shape, F32),
        compiler_params=_params(1),
        name="glu_ln",
    )(x, zt, w_t, b_col, gain, bias)


def kernel(x_prompt, x_sample, cache_k, cache_v, state_ssm_re, state_ssm_im, attn_w_qkv, attn_b_qkv, attn_sinks, attn_w_o, attn_b_o, ssm_w_in, ssm_b_in, ssm_log_dt, ssm_a_re, ssm_a_im, ssm_b_re, ssm_b_im, ssm_c_re, ssm_c_im, ssm_d, ssm_w_glu, ssm_b_glu, ffn_w_up, ffn_w_down, ln_gain, ln_bias):
    n_b, seq, _ = x_prompt.shape
    n_s, seq_s, _ = x_sample.shape
    n_p, n_d = n_b * seq, n_s * seq_s
    tm = 512
    blocks = seq // SSM_BLOCK
    half_rows = (blocks // 2) * n_b
    ssm_tm = 2 * SSM_BLOCK * n_b

    xp = x_prompt.reshape(n_p, D_MODEL)
    xs = x_sample.reshape(n_d, D_MODEL)
    tabs_p = _rope_tables(np.arange(seq))
    tabs_s = _rope_tables(np.tile(PAST_LEN + np.arange(seq_s), n_s))

    ssm_view = (n_b, 2, blocks // 2, SSM_BLOCK, D_MODEL)
    ssm_spec = pl.BlockSpec((n_b, 2, 1, SSM_BLOCK, D_MODEL), lambda i: (0, 0, i, 0, 0))
    flat_spec = pl.BlockSpec((n_d, D_MODEL), lambda i: (i, 0))

    new_k_p, new_v_p, new_re_p, new_im_p = [], [], [], []
    new_k_s, new_v_s, new_re_s, new_im_s = [], [], [], []
    for i in range(DEPTH):
        l = i // 2
        gain = ln_gain[i][:, None, :]
        bias = ln_bias[i][:, None, :]
        if i % 2 == 0:
            w_qkv_t = attn_w_qkv[l].T.astype(BF16)
            b_qkv = attn_b_qkv[l][:, None]
            w_o_t = attn_w_o[l].T.astype(BF16)
            b_o = attn_b_o[l][None]
            sink_rows = jnp.repeat(attn_sinks[l].astype(F32), PAIR).reshape(N_KV_HEADS, 1, GQA_GROUP * PAIR)
            qp, kp, vp = _qkv_rope(xp, w_qkv_t, b_qkv, tabs_p, tm, seq // tm)
            qs, ks, vs = _qkv_rope(xs, w_qkv_t, b_qkv, tabs_s, n_d, 1)
            prev_p = lambda b, t: (0, jnp.maximum(b * (seq // WINDOW) + t * (tm // WINDOW) - 1, 0))
            xp = _attn_proj_ln(xp, qp, kp, kp, vp, vp, sink_rows, w_o_t, b_o, gain[0], bias[0],
                               n_batch=n_b, tq=tm, prev_map=prev_p, has_start=True)
            pad_rows = lambda a: jnp.pad(a.reshape(n_s, seq_s, -1), ((0, 0), (0, PAIR - seq_s), (0, 0))).reshape(n_s * PAIR, -1)
            pad_cols = lambda a: jnp.pad(a.reshape(-1, n_s, seq_s), ((0, 0), (0, 0), (0, PAIR - seq_s))).reshape(-1, n_s * PAIR)
            ck_t = cache_k[l].reshape(n_s * WINDOW, KV_DIM).T
            cv_t = cache_v[l].reshape(n_s * WINDOW, KV_DIM).T
            xs = _attn_proj_ln(pad_rows(xs), pad_cols(qs), ck_t, pad_cols(ks), cv_t, pad_cols(vs), sink_rows, w_o_t, b_o,
                               gain[0], bias[0], n_batch=n_s, tq=PAIR, prev_map=lambda b, t: (0, b), has_start=False)
            xs = xs.reshape(n_s, PAIR, D_MODEL)[:, :seq_s].reshape(n_d, D_MODEL)
            last = lambda a: (a.reshape(KV_DIM, n_b, seq)[:, :, seq - WINDOW:].transpose(1, 2, 0)
                              .reshape(n_b, WINDOW, N_KV_HEADS, HEAD_DIM))
            new_k_p.append(last(kp))
            new_v_p.append(last(vp))
            ks4 = ks.T.reshape(n_s, seq_s, N_KV_HEADS, HEAD_DIM)
            vs4 = vs.T.reshape(n_s, seq_s, N_KV_HEADS, HEAD_DIM)
            new_k_s.append(jnp.concatenate([cache_k[l], ks4], axis=1)[:, -WINDOW:])
            new_v_s.append(jnp.concatenate([cache_v[l], vs4], axis=1)[:, -WINDOW:])
        else:
            w_in_t = ssm_w_in[l].T.astype(BF16)
            b_in = ssm_b_in[l][:, None]
            w_glu_t = ssm_w_glu[l].T.astype(BF16)
            b_glu = ssm_b_glu[l][:, None]
            ops = _ssm_prep(ssm_log_dt[l], ssm_a_re[l], ssm_a_im[l], ssm_b_re[l], ssm_b_im[l],
                            ssm_c_re[l], ssm_c_im[l], ssm_d[l])
            ut = _ssm_in(xp.reshape(ssm_view), ssm_spec, blocks // 2, w_in_t, b_in, ssm_tm)
            uts = _ssm_in(xs, flat_spec, 1, w_in_t, b_in, n_d)
            uts3 = jnp.pad(uts.reshape(D_MODEL, n_s, seq_s), ((0, 0), (0, 0), (0, LANES - seq_s)))
            s0_re = state_ssm_re[l].transpose(1, 0, 2)
            s0_im = state_ssm_im[l].transpose(1, 0, 2)
            zt, zs3, s_re, s_im, p_re, p_im = _ssm_core(ut, uts3, s0_re, s0_im, ops, half_rows=half_rows, n_seq=n_s)
            zst = zs3[:, :, :seq_s].reshape(D_MODEL, n_d).astype(BF16)
            xp = _glu_ln(xp.reshape(ssm_view), ssm_spec, blocks // 2, zt, w_glu_t, b_glu, gain[0], bias[0],
                         ssm_tm).reshape(n_p, D_MODEL)
            xs = _glu_ln(xs, flat_spec, 1, zst, w_glu_t, b_glu, gain[0], bias[0], n_d)
            new_re_p.append(p_re.transpose(1, 0, 2))
            new_im_p.append(p_im.transpose(1, 0, 2))
            new_re_s.append(s_re.transpose(1, 0, 2))
            new_im_s.append(s_im.transpose(1, 0, 2))
        w_up = ffn_w_up[i].astype(BF16)
        w_dn = ffn_w_down[i].astype(BF16)
        xp = _ffn_ln(xp, w_up, w_dn, gain[1], bias[1], 2 * tm)
        xs = _ffn_ln(xs, w_up, w_dn, gain[1], bias[1], n_d)
    return (xp.reshape(x_prompt.shape), xs.reshape(x_sample.shape),
            jnp.stack(new_k_p), jnp.stack(new_v_p), jnp.stack(new_re_p), jnp.stack(new_im_p),
            jnp.stack(new_k_s), jnp.stack(new_v_s), jnp.stack(new_re_s), jnp.stack(new_im_s))
```

```python
import functools

import jax
import jax.numpy as jnp
import numpy as np
from jax import lax
from jax.experimental import pallas as pl
from jax.experimental.pallas import tpu as pltpu

F32 = jnp.float32
BF16 = jnp.bfloat16

D_MODEL = 1024
DEPTH = 4
CHUNK = 64
N_HEADS = 16
N_KV_HEADS = 2
HEAD_DIM = 64
GQA_GROUP = N_HEADS // N_KV_HEADS
Q_DIM = N_HEADS * HEAD_DIM
KV_DIM = N_KV_HEADS * HEAD_DIM
QKV_DIM = Q_DIM + 2 * KV_DIM
WINDOW = 128
PAST_LEN = 4096
ROPE_THETA = 10000.0
SSM_GROUP = 16
SSM_GROUPS = D_MODEL // SSM_GROUP
SSM_STATE = 64
D_FF = 2816
DN_ALPHA = (2.0 * DEPTH) ** 0.25
LN_EPS = 1e-5
NEG_INF = -1e30
LOG2E = 1.4426950408889634

LANES = 128
SUBLANES = 8
BF16_SUBLANES = 16
FF_CHUNK = 256
N_FF_CHUNKS = D_FF // FF_CHUNK
SSM_BLOCK = CHUNK
SSM_BLOCK_DIM = SSM_GROUP * SSM_BLOCK
VMEM_LIMIT = 56 * 2 ** 20


def _params(n_axes):
    return pltpu.CompilerParams(dimension_semantics=("arbitrary",) * n_axes, vmem_limit_bytes=VMEM_LIMIT)


def _const_spec(shape):
    zeros = (0,) * len(shape)
    return pl.BlockSpec(shape, lambda *_: zeros, pipeline_mode=pl.Buffered(1))


def _sigmoid(x):
    return 0.5 * jnp.tanh(0.5 * x) + 0.5


def _gelu_tanh(x):
    k = (2.0 / np.pi) ** 0.5
    half_x = 0.5 * x
    return half_x * jnp.tanh(x * (k + (k * 0.044715) * (x * x))) + half_x


def _layer_norm(y, gain, bias):
    mu = jnp.mean(y, axis=-1, keepdims=True)
    d = y - mu
    var = jnp.mean(d * d, axis=-1, keepdims=True)
    return d * lax.rsqrt(var + LN_EPS) * gain + bias


def _ffn_body(x_ref, wup_ref, wdn_ref, g_ref, b_ref, o_ref):
    x = x_ref[...]
    xb = x.astype(BF16)
    acc = DN_ALPHA * x
    for c in range(N_FF_CHUNKS):
        cols = slice(FF_CHUNK * c, FF_CHUNK * (c + 1))
        gate = jnp.dot(xb, wup_ref[:, cols], preferred_element_type=F32)
        up = jnp.dot(xb, wup_ref[:, D_FF + FF_CHUNK * c:D_FF + FF_CHUNK * (c + 1)], preferred_element_type=F32)
        act = gate * _sigmoid(gate) * up
        acc = acc + jnp.dot(act.astype(BF16), wdn_ref[cols, :], preferred_element_type=F32)
    o_ref[...] = _layer_norm(acc, g_ref[...], b_ref[...])


def _ffn_ln(x, wup_all, wdn_all, layer, gain, bias, tm):
    n = x.shape[0]
    layer_spec = lambda *shape: pl.BlockSpec((None,) + shape, lambda i: (layer, 0, 0), pipeline_mode=pl.Buffered(1))
    return pl.pallas_call(
        _ffn_body,
        grid=(n // tm,),
        in_specs=[pl.BlockSpec((tm, D_MODEL), lambda i: (i, 0)),
                  layer_spec(D_MODEL, 2 * D_FF), layer_spec(D_FF, D_MODEL),
                  _const_spec((1, D_MODEL)), _const_spec((1, D_MODEL))],
        out_specs=pl.BlockSpec((tm, D_MODEL), lambda i: (i, 0)),
        out_shape=jax.ShapeDtypeStruct((n, D_MODEL), F32),
        compiler_params=_params(1),
        name="ffn_ln",
    )(x, wup_all, wdn_all, gain, bias)


def _qkv_body(x_ref, w_ref, b_ref, cos_ref, sin_ref, q_ref, k_ref, v_ref):
    half = HEAD_DIM // 2
    chunk = 2 * LANES
    n_chunks = x_ref.shape[0] // chunk
    nt = (((1,), (1,)), ((), ()))

    def project(h):
        xb = x_ref[h * chunk:(h + 1) * chunk, :].astype(BF16)
        return (lax.dot_general(w_ref[:Q_DIM, :], xb, nt, preferred_element_type=F32),
                lax.dot_general(w_ref[Q_DIM:, :], xb, nt, preferred_element_type=F32))

    def rope(t, cos, sin):
        rot = t.reshape(t.shape[0] // HEAD_DIM, 2, half, chunk)
        first, second = rot[:, 0], rot[:, 1]
        return jnp.stack([first * cos - second * sin, second * cos + first * sin], axis=1).reshape(t.shape)

    nxt = project(0)
    for h in range(n_chunks):
        cols = slice(h * chunk, (h + 1) * chunk)
        q_t, kv_t = nxt[0] + b_ref[:Q_DIM, :], nxt[1] + b_ref[Q_DIM:, :]
        if h + 1 < n_chunks:
            nxt = project(h + 1)
        cos, sin = cos_ref[:, cols], sin_ref[:, cols]
        q_ref[:, cols] = (rope(q_t, cos, sin) * (HEAD_DIM ** -0.5 * LOG2E)).astype(BF16)
        k_ref[:, cols] = rope(kv_t[:KV_DIM], cos, sin)
        v_ref[:, cols] = kv_t[KV_DIM:]


def _qkv_rope(x, w_t, b_col, tabs, tm, tab_tiles):
    n = x.shape[0]
    tab_spec = pl.BlockSpec((HEAD_DIM // 2, tm), lambda i: (0, i % tab_tiles))
    col = lambda rows: pl.BlockSpec((rows, tm), lambda i: (0, i))
    return pl.pallas_call(
        _qkv_body,
        grid=(n // tm,),
        in_specs=[pl.BlockSpec((tm, D_MODEL), lambda i: (i, 0)),
                  _const_spec((QKV_DIM, D_MODEL)), _const_spec((QKV_DIM, 1)), tab_spec, tab_spec],
        out_specs=[col(Q_DIM), col(KV_DIM), col(KV_DIM)],
        out_shape=[jax.ShapeDtypeStruct((Q_DIM, n), BF16),
                   jax.ShapeDtypeStruct((KV_DIM, n), F32),
                   jax.ShapeDtypeStruct((KV_DIM, n), F32)],
        compiler_params=_params(1),
        name="qkv_rope",
    )(x, w_t, b_col, *tabs)


def _rope_tables(pos):
    half = HEAD_DIM // 2
    inv = ROPE_THETA ** (-np.arange(half, dtype=np.float64) / half)
    ang = inv[:, None] * np.asarray(pos, np.float64)[None, :]
    return tuple(jnp.asarray(t.astype(np.float32)) for t in (np.cos(ang), np.sin(ang)))


PAIR = 2 * CHUNK


def _attn_body(x_ref, q_ref, kp_ref, kc_ref, vp_ref, vc_ref, sink_ref, wot_ref, bo_ref, g_ref, b_ref,
               out_ref, ot_scr, *, tq, has_start):
    span = WINDOW + PAIR
    k_t = jnp.concatenate([kp_ref[...], kc_ref[...]], axis=1)
    v_t = jnp.concatenate([vp_ref[...], vc_ref[...]], axis=1).astype(BF16)
    k_rows = [k_t[m * HEAD_DIM:(m + 1) * HEAD_DIM].T.astype(BF16) for m in range(N_KV_HEADS)]
    ones_rows = jnp.ones((BF16_SUBLANES, v_t.shape[1]), BF16)
    v_aug = [jnp.concatenate([v_t[m * HEAD_DIM:(m + 1) * HEAD_DIM], ones_rows], axis=0) for m in range(N_KV_HEADS)]
    n_cols = GQA_GROUP * PAIR
    tok = lax.broadcasted_iota(jnp.int32, (CHUNK, n_cols), 1) & (PAIR - 1)
    first_chunk = tok < CHUNK
    if has_start:
        after_start = jnp.broadcast_to(pl.program_id(1) > 0, first_chunk.shape)

    def scores(pr, m):
        heads = range(GQA_GROUP * m, GQA_GROUP * (m + 1))
        q_m = jnp.concatenate([q_ref[h * HEAD_DIM:(h + 1) * HEAD_DIM, pr * PAIR:(pr + 1) * PAIR] for h in heads], axis=1)
        return jnp.dot(k_rows[m][pr * PAIR:pr * PAIR + span], q_m, preferred_element_type=F32)

    steps = [(pr, m) for pr in range(tq // PAIR) for m in range(N_KV_HEADS)]
    proj_pairs = min(2, tq // PAIR)
    s_next = scores(*steps[0])
    for idx, (pr, m) in enumerate(steps):
        s = s_next
        if idx + 1 < len(steps):
            s_next = scores(*steps[idx + 1])
        rows = slice(pr * PAIR, (pr + 1) * PAIR)
        oldest, older = s[:CHUNK], s[CHUNK:2 * CHUNK]
        if has_start and pr == 0:
            oldest = jnp.where(after_start, oldest, NEG_INF)
            older = jnp.where(after_start, older, NEG_INF)
        edge = jnp.where(first_chunk, oldest, s[3 * CHUNK:])
        s = jnp.concatenate([edge, older, s[2 * CHUNK:3 * CHUNK]], axis=0)
        sink = sink_ref[m] * LOG2E
        mx = jnp.maximum(jnp.max(s, axis=0, keepdims=True), sink)
        e = jnp.exp2(s - mx)
        e_edge = e[:CHUNK]
        p_t = jnp.concatenate([jnp.where(first_chunk, e_edge, 0.0), e[CHUNK:],
                               jnp.where(first_chunk, 0.0, e_edge)], axis=0).astype(BF16)
        o_aug = jnp.dot(v_aug[m][:, pr * PAIR:pr * PAIR + span], p_t, preferred_element_type=F32)
        den = o_aug[HEAD_DIM:HEAD_DIM + 1] + jnp.exp2(sink - mx)
        o_t = o_aug[:HEAD_DIM] * (1.0 / den)
        for hb in range(GQA_GROUP):
            head = GQA_GROUP * m + hb
            ot_scr[head * HEAD_DIM:(head + 1) * HEAD_DIM, rows] = o_t[:, hb * PAIR:(hb + 1) * PAIR].astype(BF16)
        if m == N_KV_HEADS - 1 and (pr + 1) % proj_pairs == 0:
            blk = slice((pr + 1 - proj_pairs) * PAIR, (pr + 1) * PAIR)
            proj_t = jnp.dot(wot_ref[...], ot_scr[:, blk], preferred_element_type=F32)
            y = DN_ALPHA * x_ref[blk, :] + proj_t.T + bo_ref[...]
            out_ref[blk, :] = _layer_norm(y, g_ref[...], b_ref[...])


def _attn_proj_ln(x, q_t, k_prev, k_cur, v_prev, v_cur, sink_rows, wo_t, bo, gain, bias, *, n_batch, tq, prev_map, has_start):
    n = x.shape[0]
    tiles = n // (n_batch * tq)
    cur = lambda b, i: (b * tiles + i, 0)
    cur_t = lambda b, i: (0, b * tiles + i)
    return pl.pallas_call(
        functools.partial(_attn_body, tq=tq, has_start=has_start),
        grid=(n_batch, tiles),
        in_specs=[pl.BlockSpec((tq, D_MODEL), cur), pl.BlockSpec((Q_DIM, tq), cur_t),
                  pl.BlockSpec((KV_DIM, WINDOW), prev_map), pl.BlockSpec((KV_DIM, tq), cur_t),
                  pl.BlockSpec((KV_DIM, WINDOW), prev_map), pl.BlockSpec((KV_DIM, tq), cur_t),
                  _const_spec((N_KV_HEADS, 1, GQA_GROUP * PAIR)),
                  _const_spec((D_MODEL, Q_DIM)), _const_spec((1, D_MODEL)),
                  _const_spec((1, D_MODEL)), _const_spec((1, D_MODEL))],
        out_specs=pl.BlockSpec((tq, D_MODEL), cur),
        out_shape=jax.ShapeDtypeStruct((n, D_MODEL), F32),
        scratch_shapes=[pltpu.VMEM((Q_DIM, tq), BF16)],
        compiler_params=_params(2),
        name="attn_proj_ln",
    )(x, q_t, k_prev, k_cur, v_prev, v_cur, sink_rows, wo_t, bo, gain, bias)


def _ssm_prep_group(k, pcol_ref, prow_ref, bcol_ref, brow_ref, ccol_ref, dcol_ref, t_ref, wt_ref, v_ref, are_ref, aim_ref):
    def cmul(x_re, x_im, y_re, y_im):
        return x_re * y_re - x_im * y_im, x_re * y_im + x_im * y_re

    def a_bar(a_re, a_im, dt):
        mag = jnp.exp(dt * a_re)
        ang = dt * a_im
        return mag * jnp.cos(ang), mag * jnp.sin(ang)

    def zoh_factor(l_re, l_im, a_re, a_im):
        den = a_re * a_re + a_im * a_im
        n_re = l_re - 1.0
        return (n_re * a_re + l_im * a_im) / den, (l_im * a_re - n_re * a_im) / den

    n_bits = SSM_BLOCK.bit_length() - 1
    pc = pcol_ref[k]
    ar_c, ai_c, dt_c = pc[:, 0:1], pc[:, 1:2], jnp.exp(pc[:, 2:3])
    lam_c = a_bar(ar_c, ai_c, dt_c)
    fr_c, fi_c = zoh_factor(*lam_c, ar_c, ai_c)
    bc = bcol_ref[k]
    bbr_c = fr_c * bc[:, :SSM_GROUP] - fi_c * bc[:, SSM_GROUP:]
    bbi_c = fr_c * bc[:, SSM_GROUP:] + fi_c * bc[:, :SSM_GROUP]
    pr = prow_ref[k]
    ar_r, ai_r, dt_r = pr[0:1], pr[1:2], jnp.exp(pr[2:3])
    lam_r = a_bar(ar_r, ai_r, dt_r)
    fr_r, fi_r = zoh_factor(*lam_r, ar_r, ai_r)
    br = brow_ref[k]
    bbr_r = fr_r * br[:SSM_GROUP] - fi_r * br[SSM_GROUP:]
    bbi_r = fr_r * br[SSM_GROUP:] + fi_r * br[:SSM_GROUP]
    cc = ccol_ref[k]

    squares = [lam_c]
    for _ in range(n_bits - 1):
        squares.append(cmul(*squares[-1], *squares[-1]))

    def lane_pow(exps):
        acc = None
        for k, (q_re, q_im) in enumerate(squares):
            bit = ((exps >> k) & 1) == 1
            f_re, f_im = jnp.where(bit, q_re, 1.0), jnp.where(bit, q_im, 0.0)
            acc = (f_re, f_im) if acc is None else cmul(*acc, f_re, f_im)
        return acc

    lane = lax.broadcasted_iota(jnp.int32, (SSM_STATE, LANES), 1)
    lo = lane < SSM_BLOCK
    tau = lane & (SSM_BLOCK - 1)
    l0r, l0i = lane_pow(tau)
    l1r, l1i = cmul(l0r, l0i, *lam_c)
    lrr, lri = lane_pow(SSM_BLOCK - 1 - tau)

    g_re, g_im = [], []
    for j in range(SSM_GROUP // 2):
        sl = slice(j * LANES, (j + 1) * LANES)
        cr = jnp.where(lo, cc[:, 2 * j:2 * j + 1], cc[:, 2 * j + 1:2 * j + 2])
        ci = jnp.where(lo, cc[:, SSM_GROUP + 2 * j:SSM_GROUP + 2 * j + 1],
                       cc[:, SSM_GROUP + 2 * j + 1:SSM_GROUP + 2 * j + 2])
        g_re.append(cr * l0r - ci * l0i)
        g_im.append(-(cr * l0i + ci * l0r))
        v_ref[k, :SSM_STATE, sl] = (cr * l1r - ci * l1i).astype(BF16)
        v_ref[k, SSM_STATE:, sl] = (-(cr * l1i + ci * l1r)).astype(BF16)
        b_r = jnp.where(lo, bbr_c[:, 2 * j:2 * j + 1], bbr_c[:, 2 * j + 1:2 * j + 2])
        b_i = jnp.where(lo, bbi_c[:, 2 * j:2 * j + 1], bbi_c[:, 2 * j + 1:2 * j + 2])
        wt_ref[k, :SSM_STATE, sl] = (b_r * lrr - b_i * lri).astype(BF16)
        wt_ref[k, SSM_STATE:, sl] = (b_r * lri + b_i * lrr).astype(BF16)
    g_re = jnp.concatenate(g_re, axis=1)
    g_im = jnp.concatenate(g_im, axis=1)
    kvec = (jnp.dot(bbr_r, g_re, preferred_element_type=F32, precision=lax.Precision.HIGHEST)
            + jnp.dot(bbi_r, g_im, preferred_element_type=F32, precision=lax.Precision.HIGHEST))
    klane = lax.broadcasted_iota(jnp.int32, kvec.shape, 1)
    krow = lax.broadcasted_iota(jnp.int32, kvec.shape, 0)
    kvec = kvec + jnp.where(klane == krow * SSM_BLOCK, dcol_ref[k], 0.0)

    t_in = lax.broadcasted_iota(jnp.int32, (SSM_BLOCK, LANES), 0)
    causal = (lax.broadcasted_iota(jnp.int32, (SSM_BLOCK, LANES), 1) & (SSM_BLOCK - 1)) >= t_in
    for c in range(SSM_GROUP):
        for j in range(SSM_GROUP // 2):
            src = jnp.broadcast_to(kvec[c:c + 1, j * LANES:(j + 1) * LANES], (SSM_BLOCK, LANES))
            shifted = pltpu.roll(src, 0, 1, stride=1, stride_axis=0)
            t_ref[k, c * SSM_BLOCK:(c + 1) * SSM_BLOCK, j * LANES:(j + 1) * LANES] = (
                jnp.where(causal, shifted, 0.0).astype(BF16))
    a_re, a_im = lam_r
    for _ in range(n_bits):
        a_re, a_im = cmul(a_re, a_im, a_re, a_im)
    are_ref[k] = a_re
    aim_ref[k] = a_im


def _ssm_prep_inputs(log_dt, a_re, a_im, b_re, b_im, c_re, c_im, d):
    g, p, cg = SSM_GROUPS, SSM_STATE, SSM_GROUP
    params = jnp.stack([a_re, a_im, jnp.broadcast_to(log_dt[:, None], (g, p))], axis=-1)
    bcol = jnp.concatenate([b_re, b_im], axis=-1)
    ccol = jnp.concatenate([c_re.transpose(0, 2, 1), c_im.transpose(0, 2, 1)], axis=-1)
    return params, params.transpose(0, 2, 1), bcol, bcol.transpose(0, 2, 1), ccol, d.reshape(g, cg, 1)


def _ssm_in_body(x_ref, w_ref, b_ref, *rest, tm, groups):
    prep_refs, o_ref = rest[:-1 - 5 * bool(groups)], rest[-1 - 5 * bool(groups)]
    xb = x_ref[...].reshape(tm, D_MODEL).astype(BF16)
    ut = lax.dot_general(w_ref[...], xb, (((1,), (1,)), ((), ())), preferred_element_type=F32)
    o_ref[...] = ut + b_ref[...]
    for k in range(groups):
        _ssm_prep_group(k, *prep_refs, *rest[-5:])


def _ssm_in(x, x_spec, n_tiles, w_t, b_col, tm, prep_inputs=()):
    g, p, cg, n = SSM_GROUPS, SSM_STATE, SSM_GROUP, SSM_BLOCK_DIM
    groups = g // n_tiles if prep_inputs else 0
    spec = lambda *shape: pl.BlockSpec((groups,) + shape, lambda i: (i, 0, 0))
    prep_in = [spec(p, 3), spec(3, p), spec(p, 2 * cg), spec(2 * cg, p), spec(p, 2 * cg), spec(cg, 1)]
    prep_out = [spec(n, n), spec(2 * p, n), spec(2 * p, n), spec(1, p), spec(1, p)]
    prep_shapes = ([jax.ShapeDtypeStruct((g, n, n), BF16)] + [jax.ShapeDtypeStruct((g, 2 * p, n), BF16)] * 2
                   + [jax.ShapeDtypeStruct((g, 1, p), F32)] * 2)
    return pl.pallas_call(
        functools.partial(_ssm_in_body, tm=tm, groups=groups),
        grid=(n_tiles,),
        in_specs=[x_spec, _const_spec((D_MODEL, D_MODEL)), _const_spec((D_MODEL, 1))] + (prep_in if groups else []),
        out_specs=[pl.BlockSpec((D_MODEL, tm), lambda i: (0, i))] + (prep_out if groups else []),
        out_shape=[jax.ShapeDtypeStruct((D_MODEL, n_tiles * tm), F32)] + (prep_shapes if groups else []),
        compiler_params=_params(1),
        name="ssm_in",
    )(x, w_t, b_col, *prep_inputs)


def _swap_sublanes_with_slabs(slabs):
    sub = lax.broadcasted_iota(jnp.int32, slabs[0].shape, 1)
    for k in range(3):
        step = 1 << k
        upper = (sub & step) != 0
        nxt = list(slabs)
        for j in range(SUBLANES):
            if j & step:
                continue
            a_lo, a_hi = slabs[j], slabs[j | step]
            nxt[j] = jnp.where(upper, pltpu.roll(a_hi, step, 1), a_lo)
            nxt[j | step] = jnp.where(upper, a_hi, pltpu.roll(a_lo, SUBLANES - step, 1))
        slabs = nxt
    return slabs


def _rows_to_planes(x8, n_rows):
    q_n = n_rows // SUBLANES
    slabs = [jnp.stack([x8[:, (SUBLANES * q + j) * LANES:(SUBLANES * q + j + 1) * LANES] for q in range(q_n)])
             for j in range(SUBLANES)]
    return [p.reshape(n_rows, LANES) for p in _swap_sublanes_with_slabs(slabs)]


def _planes_to_rows(planes, n_rows):
    q_n = n_rows // SUBLANES
    slabs = _swap_sublanes_with_slabs([p.reshape(q_n, SUBLANES, LANES) for p in planes])
    return jnp.concatenate([slabs[j][q] for q in range(q_n) for j in range(SUBLANES)], axis=1)


def _ssm_core_body(x_ref, xs_ref, s0re_ref, s0im_ref, t_ref, wt_ref, v_ref, are_ref, aim_ref,
                   z_ref, zs_ref, sre_ref, sim_ref, pre_ref, pim_ref, s_in, *, half_rows, n_seq):
    n_prompt = 2 * half_rows
    n_rows = s_in.shape[0]
    lo = lax.broadcasted_iota(jnp.int32, (half_rows, LANES), 1) < SSM_BLOCK
    lo_s = lax.broadcasted_iota(jnp.int32, (n_seq, LANES), 1) < SSM_BLOCK
    pad = jnp.zeros((n_rows - n_prompt - n_seq, LANES), F32)
    planes = []
    for h in range(SSM_GROUP // SUBLANES):
        planes += _rows_to_planes(x_ref[SUBLANES * h:SUBLANES * (h + 1), :], half_rows)
    cols = []
    z_planes = [None] * SSM_GROUP
    for j in range(SSM_GROUP // 2):
        xa = planes[2 * j]
        xb = planes[2 * j + 1]
        first = jnp.where(lo, xa, pltpu.roll(xb, SSM_BLOCK, 1))
        second = jnp.where(lo, pltpu.roll(xa, SSM_BLOCK, 1), xb)
        samp = jnp.where(lo_s, xs_ref[2 * j], pltpu.roll(xs_ref[2 * j + 1], SSM_BLOCK, 1))
        cols.append(jnp.concatenate([first, second, samp, pad], axis=0).astype(BF16))
    u = jnp.concatenate(cols, axis=1)

    inj = lax.dot_general(u, wt_ref[0], (((1,), (1,)), ((), ())), preferred_element_type=F32)
    inj_re, inj_im = inj[:, :SSM_STATE], inj[:, SSM_STATE:]
    y_u = jnp.dot(u, t_ref[0], preferred_element_type=F32)
    a_re, a_im = are_ref[0], aim_ref[0]

    s_re = jnp.zeros((n_seq, SSM_STATE), F32)
    s_im = jnp.zeros((n_seq, SSM_STATE), F32)
    for n in range(n_prompt // n_seq):
        r = slice(n * n_seq, (n + 1) * n_seq)
        s_in[r, :SSM_STATE] = s_re
        s_in[r, SSM_STATE:] = s_im
        s_re, s_im = (a_re * s_re - a_im * s_im + inj_re[r], a_re * s_im + a_im * s_re + inj_im[r])
    pre_ref[0] = s_re
    pim_ref[0] = s_im
    rs = slice(n_prompt, n_prompt + n_seq)
    s0_re, s0_im = s0re_ref[0], s0im_ref[0]
    s_in[rs, :SSM_STATE] = s0_re
    s_in[rs, SSM_STATE:] = s0_im
    s_in[n_prompt + n_seq:, :] = jnp.zeros((n_rows - n_prompt - n_seq, 2 * SSM_STATE), F32)
    sre_ref[0] = a_re * s0_re - a_im * s0_im + inj_re[rs]
    sim_ref[0] = a_re * s0_im + a_im * s0_re + inj_im[rs]

    s_bf = s_in[...].astype(BF16)
    width = 2 * LANES
    for h in range(SSM_BLOCK_DIM // width):
        y = y_u[:, h * width:(h + 1) * width] + jnp.dot(s_bf, v_ref[0, :, h * width:(h + 1) * width],
                                                        preferred_element_type=F32)
        z = _gelu_tanh(y)
        for jj in range(width // LANES):
            j = h * (width // LANES) + jj
            blk = z[:, jj * LANES:(jj + 1) * LANES]
            first, second = blk[:half_rows], blk[half_rows:n_prompt]
            z_planes[2 * j] = jnp.where(lo, first, pltpu.roll(second, SSM_BLOCK, 1))
            z_planes[2 * j + 1] = jnp.where(lo, pltpu.roll(first, SSM_BLOCK, 1), second)
            samp = blk[rs]
            zs_ref[2 * j] = samp
            zs_ref[2 * j + 1] = pltpu.roll(samp, SSM_BLOCK, 1)
    z_ref[...] = jnp.concatenate([_planes_to_rows(z_planes[SUBLANES * h:SUBLANES * (h + 1)], half_rows)
                                  for h in range(SSM_GROUP // SUBLANES)], axis=0).astype(z_ref.dtype)


def _ssm_core(ut, uts3, s0_re, s0_im, ops, *, half_rows, n_seq):
    t_op, wt, v_op, a_re, a_im = ops
    g, p, cg, n = SSM_GROUPS, SSM_STATE, SSM_GROUP, SSM_BLOCK_DIM
    n_rows = 2 * half_rows + 2 * n_seq
    grp = lambda *shape: pl.BlockSpec((1,) + shape, lambda i: (i, 0, 0))
    chan = lambda r: pl.BlockSpec((cg, r, LANES), lambda i: (i, 0, 0))
    rows = pl.BlockSpec((cg, half_rows * LANES), lambda i: (i, 0))
    state = jax.ShapeDtypeStruct((g, n_seq, p), F32)
    return pl.pallas_call(
        functools.partial(_ssm_core_body, half_rows=half_rows, n_seq=n_seq),
        grid=(g,),
        in_specs=[rows, chan(n_seq), grp(n_seq, p), grp(n_seq, p),
                  grp(n, n), grp(2 * p, n), grp(2 * p, n), grp(1, p), grp(1, p)],
        out_specs=[rows, chan(n_seq), grp(n_seq, p), grp(n_seq, p), grp(n_seq, p), grp(n_seq, p)],
        out_shape=[jax.ShapeDtypeStruct(ut.shape, BF16), jax.ShapeDtypeStruct(uts3.shape, F32),
                   state, state, state, state],
        scratch_shapes=[pltpu.VMEM((n_rows, 2 * p), F32)],
        compiler_params=_params(1),
        name="ssm_core",
    )(ut, uts3, s0_re, s0_im, t_op, wt, v_op, a_re, a_im)


def _glu_body(x_ref, zt_ref, w_ref, b_ref, g_ref, bb_ref, o_ref, *, tm):
    chunk = 2 * LANES
    lead = chunk // (tm // x_ref.shape[0])

    def gate_values(h):
        zt = zt_ref[:, h * chunk:(h + 1) * chunk]
        return (jnp.dot(w_ref[:D_MODEL, :], zt, preferred_element_type=F32),
                jnp.dot(w_ref[D_MODEL:, :], zt, preferred_element_type=F32))

    gv_next = gate_values(0)
    for h in range(tm // chunk):
        value, gate = gv_next[0] + b_ref[:D_MODEL, :], gv_next[1] + b_ref[D_MODEL:, :]
        if h + 1 < tm // chunk:
            gv_next = gate_values(h + 1)
        mixed = value * _sigmoid(gate)
        x = x_ref[h * lead:(h + 1) * lead].reshape(chunk, D_MODEL)
        y = DN_ALPHA * x + mixed.T
        o_ref[h * lead:(h + 1) * lead] = _layer_norm(y, g_ref[...], bb_ref[...]).reshape((lead,) + o_ref.shape[1:])


def _glu_ln(x, x_spec, n_tiles, zt, w_t, b_col, gain, bias, tm):
    return pl.pallas_call(
        functools.partial(_glu_body, tm=tm),
        grid=(n_tiles,),
        in_specs=[x_spec, pl.BlockSpec((D_MODEL, tm), lambda i: (0, i)),
                  _const_spec((2 * D_MODEL, D_MODEL)), _const_spec((2 * D_MODEL, 1)),
                  _const_spec((1, D_MODEL)), _const_spec((1, D_MODEL))],
        out_specs=x_spec,
        out_shape=jax.ShapeDtypeStruct(x.shape, F32),
        compiler_params=_params(1),
        name="glu_ln",
    )(x, zt, w_t, b_col, gain, bias)


def kernel(x_prompt, x_sample, cache_k, cache_v, state_ssm_re, state_ssm_im, attn_w_qkv, attn_b_qkv, attn_sinks, attn_w_o, attn_b_o, ssm_w_in, ssm_b_in, ssm_log_dt, ssm_a_re, ssm_a_im, ssm_b_re, ssm_b_im, ssm_c_re, ssm_c_im, ssm_d, ssm_w_glu, ssm_b_glu, ffn_w_up, ffn_w_down, ln_gain, ln_bias):
    n_b, seq, _ = x_prompt.shape
    n_s, seq_s, _ = x_sample.shape
    n_p, n_d = n_b * seq, n_s * seq_s
    tm = 512
    blocks = seq // SSM_BLOCK
    half_rows = (blocks // 2) * n_b
    ssm_tm = 2 * SSM_BLOCK * n_b

    xp = x_prompt.reshape(n_p, D_MODEL)
    xs = x_sample.reshape(n_d, D_MODEL)
    tabs_p = _rope_tables(np.arange(seq))
    tabs_s = _rope_tables(np.tile(PAST_LEN + np.arange(seq_s), n_s))

    ssm_view = (n_b, 2, blocks // 2, SSM_BLOCK, D_MODEL)
    ssm_spec = pl.BlockSpec((n_b, 2, 1, SSM_BLOCK, D_MODEL), lambda i: (0, 0, i, 0, 0))
    flat_spec = pl.BlockSpec((n_d, D_MODEL), lambda i: (i, 0))

    w_up_all = ffn_w_up.astype(BF16)
    w_dn_all = ffn_w_down.astype(BF16)
    new_k_p, new_v_p, new_re_p, new_im_p = [], [], [], []
    new_k_s, new_v_s, new_re_s, new_im_s = [], [], [], []
    for i in range(DEPTH):
        l = i // 2
        gain = ln_gain[i][:, None, :]
        bias = ln_bias[i][:, None, :]
        if i % 2 == 0:
            w_qkv_t = attn_w_qkv[l].T.astype(BF16)
            b_qkv = attn_b_qkv[l][:, None]
            w_o_t = attn_w_o[l].T.astype(BF16)
            b_o = attn_b_o[l][None]
            sink_rows = jnp.repeat(attn_sinks[l].astype(F32), PAIR).reshape(N_KV_HEADS, 1, GQA_GROUP * PAIR)
            qp, kp, vp = _qkv_rope(xp, w_qkv_t, b_qkv, tabs_p, tm, seq // tm)
            qs, ks, vs = _qkv_rope(xs, w_qkv_t, b_qkv, tabs_s, n_d, 1)
            prev_p = lambda b, t: (0, jnp.maximum(b * (seq // WINDOW) + t * (tm // WINDOW) - 1, 0))
            xp = _attn_proj_ln(xp, qp, kp, kp, vp, vp, sink_rows, w_o_t, b_o, gain[0], bias[0],
                               n_batch=n_b, tq=tm, prev_map=prev_p, has_start=True)
            pad_rows = lambda a: jnp.pad(a.reshape(n_s, seq_s, -1), ((0, 0), (0, PAIR - seq_s), (0, 0))).reshape(n_s * PAIR, -1)
            pad_cols = lambda a: jnp.pad(a.reshape(-1, n_s, seq_s), ((0, 0), (0, 0), (0, PAIR - seq_s))).reshape(-1, n_s * PAIR)
            ck_t = cache_k[l].reshape(n_s * WINDOW, KV_DIM).T
            cv_t = cache_v[l].reshape(n_s * WINDOW, KV_DIM).T
            xs = _attn_proj_ln(pad_rows(xs), pad_cols(qs), ck_t, pad_cols(ks), cv_t, pad_cols(vs), sink_rows, w_o_t, b_o,
                               gain[0], bias[0], n_batch=n_s, tq=PAIR, prev_map=lambda b, t: (0, b), has_start=False)
            xs = xs.reshape(n_s, PAIR, D_MODEL)[:, :seq_s].reshape(n_d, D_MODEL)
            last = lambda a: (a.reshape(KV_DIM, n_b, seq)[:, :, seq - WINDOW:].transpose(1, 2, 0)
                              .reshape(n_b, WINDOW, N_KV_HEADS, HEAD_DIM))
            new_k_p.append(last(kp))
            new_v_p.append(last(vp))
            ks4 = ks.T.reshape(n_s, seq_s, N_KV_HEADS, HEAD_DIM)
            vs4 = vs.T.reshape(n_s, seq_s, N_KV_HEADS, HEAD_DIM)
            new_k_s.append(jnp.concatenate([cache_k[l], ks4], axis=1)[:, -WINDOW:])
            new_v_s.append(jnp.concatenate([cache_v[l], vs4], axis=1)[:, -WINDOW:])
        else:
            w_in_t = ssm_w_in[l].T.astype(BF16)
            b_in = ssm_b_in[l][:, None]
            w_glu_t = ssm_w_glu[l].T.astype(BF16)
            b_glu = ssm_b_glu[l][:, None]
            prep_inputs = _ssm_prep_inputs(ssm_log_dt[l], ssm_a_re[l], ssm_a_im[l], ssm_b_re[l], ssm_b_im[l],
                                           ssm_c_re[l], ssm_c_im[l], ssm_d[l])
            ut, *ops = _ssm_in(xp.reshape(ssm_view), ssm_spec, blocks // 2, w_in_t, b_in, ssm_tm, prep_inputs)
            uts, = _ssm_in(xs, flat_spec, 1, w_in_t, b_in, n_d)
            uts3 = jnp.pad(uts.reshape(D_MODEL, n_s, seq_s), ((0, 0), (0, 0), (0, LANES - seq_s)))
            s0_re = state_ssm_re[l].transpose(1, 0, 2)
            s0_im = state_ssm_im[l].transpose(1, 0, 2)
            zt, zs3, s_re, s_im, p_re, p_im = _ssm_core(ut, uts3, s0_re, s0_im, ops, half_rows=half_rows, n_seq=n_s)
            zst = zs3[:, :, :seq_s].reshape(D_MODEL, n_d).astype(BF16)
            xp = _glu_ln(xp.reshape(ssm_view), ssm_spec, blocks // 2, zt, w_glu_t, b_glu, gain[0], bias[0],
                         ssm_tm).reshape(n_p, D_MODEL)
            xs = _glu_ln(xs, flat_spec, 1, zst, w_glu_t, b_glu, gain[0], bias[0], n_d)
            new_re_p.append(p_re.transpose(1, 0, 2))
            new_im_p.append(p_im.transpose(1, 0, 2))
            new_re_s.append(s_re.transpose(1, 0, 2))
            new_im_s.append(s_im.transpose(1, 0, 2))
        xp = _ffn_ln(xp, w_up_all, w_dn_all, i, gain[1], bias[1], 2 * tm)
        xs = _ffn_ln(xs, w_up_all, w_dn_all, i, gain[1], bias[1], n_d)
    return (xp.reshape(x_prompt.shape), xs.reshape(x_sample.shape),
            jnp.stack(new_k_p), jnp.stack(new_v_p), jnp.stack(new_re_p), jnp.stack(new_im_p),
            jnp.stack(new_k_s), jnp.stack(new_v_s), jnp.stack(new_re_s), jnp.stack(new_im_s))
```

```python
import functools

import jax
import jax.numpy as jnp
import numpy as np
from jax import lax
from jax.experimental import pallas as pl
from jax.experimental.pallas import tpu as pltpu

F32 = jnp.float32
BF16 = jnp.bfloat16

D_MODEL = 1024
DEPTH = 4
CHUNK = 64
N_HEADS = 16
N_KV_HEADS = 2
HEAD_DIM = 64
GQA_GROUP = N_HEADS // N_KV_HEADS
Q_DIM = N_HEADS * HEAD_DIM
KV_DIM = N_KV_HEADS * HEAD_DIM
QKV_DIM = Q_DIM + 2 * KV_DIM
WINDOW = 128
PAST_LEN = 4096
ROPE_THETA = 10000.0
SSM_GROUP = 16
SSM_GROUPS = D_MODEL // SSM_GROUP
SSM_STATE = 64
D_FF = 2816
DN_ALPHA = (2.0 * DEPTH) ** 0.25
LN_EPS = 1e-5
NEG_INF = -1e30
LOG2E = 1.4426950408889634

LANES = 128
SUBLANES = 8
BF16_SUBLANES = 16
FF_CHUNK = 256
N_FF_CHUNKS = D_FF // FF_CHUNK
SSM_BLOCK = CHUNK
SSM_BLOCK_DIM = SSM_GROUP * SSM_BLOCK
SSM_GROUPS_PER_STEP = 2
VMEM_LIMIT = 56 * 2 ** 20


def _params(n_axes):
    return pltpu.CompilerParams(dimension_semantics=("arbitrary",) * n_axes, vmem_limit_bytes=VMEM_LIMIT)


def _const_spec(shape):
    zeros = (0,) * len(shape)
    return pl.BlockSpec(shape, lambda *_: zeros, pipeline_mode=pl.Buffered(1))


def _sigmoid(x):
    return 0.5 * jnp.tanh(0.5 * x) + 0.5


def _gelu_tanh(x):
    k = (2.0 / np.pi) ** 0.5
    half_x = 0.5 * x
    return half_x * jnp.tanh(x * (k + (k * 0.044715) * (x * x))) + half_x


def _layer_norm(y, gain, bias):
    mu = jnp.mean(y, axis=-1, keepdims=True)
    d = y - mu
    var = jnp.mean(d * d, axis=-1, keepdims=True)
    return d * lax.rsqrt(var + LN_EPS) * gain + bias


def _ffn_body(x_ref, wup_ref, wdn_ref, g_ref, b_ref, o_ref):
    x = x_ref[...]
    xb = x.astype(BF16)
    acc = DN_ALPHA * x
    for c in range(N_FF_CHUNKS):
        cols = slice(FF_CHUNK * c, FF_CHUNK * (c + 1))
        gate = jnp.dot(xb, wup_ref[:, cols], preferred_element_type=F32)
        up = jnp.dot(xb, wup_ref[:, D_FF + FF_CHUNK * c:D_FF + FF_CHUNK * (c + 1)], preferred_element_type=F32)
        act = gate * _sigmoid(gate) * up
        acc = acc + jnp.dot(act.astype(BF16), wdn_ref[cols, :], preferred_element_type=F32)
    o_ref[...] = _layer_norm(acc, g_ref[...], b_ref[...])


def _ffn_ln(x, wup_all, wdn_all, layer, gain, bias, tm):
    n = x.shape[0]
    layer_spec = lambda *shape: pl.BlockSpec((None,) + shape, lambda i: (layer, 0, 0), pipeline_mode=pl.Buffered(1))
    return pl.pallas_call(
        _ffn_body,
        grid=(n // tm,),
        in_specs=[pl.BlockSpec((tm, D_MODEL), lambda i: (i, 0)),
                  layer_spec(D_MODEL, 2 * D_FF), layer_spec(D_FF, D_MODEL),
                  _const_spec((1, D_MODEL)), _const_spec((1, D_MODEL))],
        out_specs=pl.BlockSpec((tm, D_MODEL), lambda i: (i, 0)),
        out_shape=jax.ShapeDtypeStruct((n, D_MODEL), F32),
        compiler_params=_params(1),
        name="ffn_ln",
    )(x, wup_all, wdn_all, gain, bias)


def _qkv_body(x_ref, w_ref, b_ref, cos_ref, sin_ref, q_ref, k_ref, v_ref):
    half = HEAD_DIM // 2
    chunk = 2 * LANES
    n_chunks = x_ref.shape[0] // chunk
    nt = (((1,), (1,)), ((), ()))

    def project(h):
        xb = x_ref[h * chunk:(h + 1) * chunk, :].astype(BF16)
        return (lax.dot_general(w_ref[:Q_DIM, :], xb, nt, preferred_element_type=F32),
                lax.dot_general(w_ref[Q_DIM:, :], xb, nt, preferred_element_type=F32))

    def rope(t, cos, sin):
        rot = t.reshape(t.shape[0] // HEAD_DIM, 2, half, chunk)
        first, second = rot[:, 0], rot[:, 1]
        return jnp.stack([first * cos - second * sin, second * cos + first * sin], axis=1).reshape(t.shape)

    nxt = project(0)
    for h in range(n_chunks):
        cols = slice(h * chunk, (h + 1) * chunk)
        q_t, kv_t = nxt[0] + b_ref[:Q_DIM, :], nxt[1] + b_ref[Q_DIM:, :]
        if h + 1 < n_chunks:
            nxt = project(h + 1)
        cos, sin = cos_ref[:, cols], sin_ref[:, cols]
        q_ref[:, cols] = (rope(q_t, cos, sin) * (HEAD_DIM ** -0.5 * LOG2E)).astype(BF16)
        k_ref[:, cols] = rope(kv_t[:KV_DIM], cos, sin)
        v_ref[:, cols] = kv_t[KV_DIM:]


def _qkv_rope(x, w_t, b_col, tabs, tm, tab_tiles):
    n = x.shape[0]
    tab_spec = pl.BlockSpec((HEAD_DIM // 2, tm), lambda i: (0, i % tab_tiles))
    col = lambda rows: pl.BlockSpec((rows, tm), lambda i: (0, i))
    return pl.pallas_call(
        _qkv_body,
        grid=(n // tm,),
        in_specs=[pl.BlockSpec((tm, D_MODEL), lambda i: (i, 0)),
                  _const_spec((QKV_DIM, D_MODEL)), _const_spec((QKV_DIM, 1)), tab_spec, tab_spec],
        out_specs=[col(Q_DIM), col(KV_DIM), col(KV_DIM)],
        out_shape=[jax.ShapeDtypeStruct((Q_DIM, n), BF16),
                   jax.ShapeDtypeStruct((KV_DIM, n), F32),
                   jax.ShapeDtypeStruct((KV_DIM, n), F32)],
        compiler_params=_params(1),
        name="qkv_rope",
    )(x, w_t, b_col, *tabs)


def _rope_tables(pos):
    half = HEAD_DIM // 2
    inv = ROPE_THETA ** (-np.arange(half, dtype=np.float64) / half)
    ang = inv[:, None] * np.asarray(pos, np.float64)[None, :]
    return tuple(jnp.asarray(t.astype(np.float32)) for t in (np.cos(ang), np.sin(ang)))


PAIR = 2 * CHUNK


def _attn_body(x_ref, q_ref, kp_ref, kc_ref, vp_ref, vc_ref, sink_ref, wot_ref, bo_ref, g_ref, b_ref,
               out_ref, ot_scr, *, tq, has_start):
    span = WINDOW + PAIR
    k_t = jnp.concatenate([kp_ref[...], kc_ref[...]], axis=1)
    v_t = jnp.concatenate([vp_ref[...], vc_ref[...]], axis=1).astype(BF16)
    k_rows = [k_t[m * HEAD_DIM:(m + 1) * HEAD_DIM].T.astype(BF16) for m in range(N_KV_HEADS)]
    ones_rows = jnp.ones((BF16_SUBLANES, v_t.shape[1]), BF16)
    v_aug = [jnp.concatenate([v_t[m * HEAD_DIM:(m + 1) * HEAD_DIM], ones_rows], axis=0) for m in range(N_KV_HEADS)]
    n_cols = GQA_GROUP * PAIR
    tok = lax.broadcasted_iota(jnp.int32, (CHUNK, n_cols), 1) & (PAIR - 1)
    first_chunk = tok < CHUNK
    if has_start:
        after_start = jnp.broadcast_to(pl.program_id(1) > 0, first_chunk.shape)

    def scores(pr, m):
        heads = range(GQA_GROUP * m, GQA_GROUP * (m + 1))
        q_m = jnp.concatenate([q_ref[h * HEAD_DIM:(h + 1) * HEAD_DIM, pr * PAIR:(pr + 1) * PAIR] for h in heads], axis=1)
        return jnp.dot(k_rows[m][pr * PAIR:pr * PAIR + span], q_m, preferred_element_type=F32)

    steps = [(pr, m) for pr in range(tq // PAIR) for m in range(N_KV_HEADS)]
    proj_pairs = min(2, tq // PAIR)
    s_next = scores(*steps[0])
    for idx, (pr, m) in enumerate(steps):
        s = s_next
        if idx + 1 < len(steps):
            s_next = scores(*steps[idx + 1])
        rows = slice(pr * PAIR, (pr + 1) * PAIR)
        oldest, older = s[:CHUNK], s[CHUNK:2 * CHUNK]
        if has_start and pr == 0:
            oldest = jnp.where(after_start, oldest, NEG_INF)
            older = jnp.where(after_start, older, NEG_INF)
        edge = jnp.where(first_chunk, oldest, s[3 * CHUNK:])
        s = jnp.concatenate([edge, older, s[2 * CHUNK:3 * CHUNK]], axis=0)
        sink = sink_ref[m] * LOG2E
        mx = jnp.maximum(jnp.max(s, axis=0, keepdims=True), sink)
        e = jnp.exp2(s - mx)
        e_edge = e[:CHUNK]
        p_t = jnp.concatenate([jnp.where(first_chunk, e_edge, 0.0), e[CHUNK:],
                               jnp.where(first_chunk, 0.0, e_edge)], axis=0).astype(BF16)
        o_aug = jnp.dot(v_aug[m][:, pr * PAIR:pr * PAIR + span], p_t, preferred_element_type=F32)
        den = o_aug[HEAD_DIM:HEAD_DIM + 1] + jnp.exp2(sink - mx)
        o_t = o_aug[:HEAD_DIM] * (1.0 / den)
        for hb in range(GQA_GROUP):
            head = GQA_GROUP * m + hb
            ot_scr[head * HEAD_DIM:(head + 1) * HEAD_DIM, rows] = o_t[:, hb * PAIR:(hb + 1) * PAIR].astype(BF16)
        if m == N_KV_HEADS - 1 and (pr + 1) % proj_pairs == 0:
            blk = slice((pr + 1 - proj_pairs) * PAIR, (pr + 1) * PAIR)
            proj_t = jnp.dot(wot_ref[...], ot_scr[:, blk], preferred_element_type=F32)
            y = DN_ALPHA * x_ref[blk, :] + proj_t.T + bo_ref[...]
            out_ref[blk, :] = _layer_norm(y, g_ref[...], b_ref[...])


def _attn_proj_ln(x, q_t, k_prev, k_cur, v_prev, v_cur, sink_rows, wo_t, bo, gain, bias, *, n_batch, tq, prev_map, has_start):
    n = x.shape[0]
    tiles = n // (n_batch * tq)
    cur = lambda b, i: (b * tiles + i, 0)
    cur_t = lambda b, i: (0, b * tiles + i)
    return pl.pallas_call(
        functools.partial(_attn_body, tq=tq, has_start=has_start),
        grid=(n_batch, tiles),
        in_specs=[pl.BlockSpec((tq, D_MODEL), cur), pl.BlockSpec((Q_DIM, tq), cur_t),
                  pl.BlockSpec((KV_DIM, WINDOW), prev_map), pl.BlockSpec((KV_DIM, tq), cur_t),
                  pl.BlockSpec((KV_DIM, WINDOW), prev_map), pl.BlockSpec((KV_DIM, tq), cur_t),
                  _const_spec((N_KV_HEADS, 1, GQA_GROUP * PAIR)),
                  _const_spec((D_MODEL, Q_DIM)), _const_spec((1, D_MODEL)),
                  _const_spec((1, D_MODEL)), _const_spec((1, D_MODEL))],
        out_specs=pl.BlockSpec((tq, D_MODEL), cur),
        out_shape=jax.ShapeDtypeStruct((n, D_MODEL), F32),
        scratch_shapes=[pltpu.VMEM((Q_DIM, tq), BF16)],
        compiler_params=_params(2),
        name="attn_proj_ln",
    )(x, q_t, k_prev, k_cur, v_prev, v_cur, sink_rows, wo_t, bo, gain, bias)


def _ssm_prep_group(k, pcol_ref, prow_ref, bcol_ref, brow_ref, ccol_ref, dcol_ref, t_ref, wt_ref, v_ref, are_ref, aim_ref):
    def cmul(x_re, x_im, y_re, y_im):
        return x_re * y_re - x_im * y_im, x_re * y_im + x_im * y_re

    def a_bar(a_re, a_im, dt):
        mag = jnp.exp(dt * a_re)
        ang = dt * a_im
        return mag * jnp.cos(ang), mag * jnp.sin(ang)

    def zoh_factor(l_re, l_im, a_re, a_im):
        den = a_re * a_re + a_im * a_im
        n_re = l_re - 1.0
        return (n_re * a_re + l_im * a_im) / den, (l_im * a_re - n_re * a_im) / den

    n_bits = SSM_BLOCK.bit_length() - 1
    pc = pcol_ref[k]
    ar_c, ai_c, dt_c = pc[:, 0:1], pc[:, 1:2], jnp.exp(pc[:, 2:3])
    lam_c = a_bar(ar_c, ai_c, dt_c)
    fr_c, fi_c = zoh_factor(*lam_c, ar_c, ai_c)
    bc = bcol_ref[k]
    bbr_c = fr_c * bc[:, :SSM_GROUP] - fi_c * bc[:, SSM_GROUP:]
    bbi_c = fr_c * bc[:, SSM_GROUP:] + fi_c * bc[:, :SSM_GROUP]
    pr = prow_ref[k]
    ar_r, ai_r, dt_r = pr[0:1], pr[1:2], jnp.exp(pr[2:3])
    lam_r = a_bar(ar_r, ai_r, dt_r)
    fr_r, fi_r = zoh_factor(*lam_r, ar_r, ai_r)
    br = brow_ref[k]
    bbr_r = fr_r * br[:SSM_GROUP] - fi_r * br[SSM_GROUP:]
    bbi_r = fr_r * br[SSM_GROUP:] + fi_r * br[:SSM_GROUP]
    cc = ccol_ref[k]

    squares = [lam_c]
    for _ in range(n_bits - 1):
        squares.append(cmul(*squares[-1], *squares[-1]))

    def lane_pow(exps):
        acc = None
        for k, (q_re, q_im) in enumerate(squares):
            bit = ((exps >> k) & 1) == 1
            f_re, f_im = jnp.where(bit, q_re, 1.0), jnp.where(bit, q_im, 0.0)
            acc = (f_re, f_im) if acc is None else cmul(*acc, f_re, f_im)
        return acc

    lane = lax.broadcasted_iota(jnp.int32, (SSM_STATE, LANES), 1)
    lo = lane < SSM_BLOCK
    tau = lane & (SSM_BLOCK - 1)
    l0r, l0i = lane_pow(tau)
    l1r, l1i = cmul(l0r, l0i, *lam_c)
    lrr, lri = lane_pow(SSM_BLOCK - 1 - tau)

    g_re, g_im = [], []
    for j in range(SSM_GROUP // 2):
        sl = slice(j * LANES, (j + 1) * LANES)
        cr = jnp.where(lo, cc[:, 2 * j:2 * j + 1], cc[:, 2 * j + 1:2 * j + 2])
        ci = jnp.where(lo, cc[:, SSM_GROUP + 2 * j:SSM_GROUP + 2 * j + 1],
                       cc[:, SSM_GROUP + 2 * j + 1:SSM_GROUP + 2 * j + 2])
        g_re.append(cr * l0r - ci * l0i)
        g_im.append(-(cr * l0i + ci * l0r))
        v_ref[k, :SSM_STATE, sl] = (cr * l1r - ci * l1i).astype(BF16)
        v_ref[k, SSM_STATE:, sl] = (-(cr * l1i + ci * l1r)).astype(BF16)
        b_r = jnp.where(lo, bbr_c[:, 2 * j:2 * j + 1], bbr_c[:, 2 * j + 1:2 * j + 2])
        b_i = jnp.where(lo, bbi_c[:, 2 * j:2 * j + 1], bbi_c[:, 2 * j + 1:2 * j + 2])
        wt_ref[k, :SSM_STATE, sl] = (b_r * lrr - b_i * lri).astype(BF16)
        wt_ref[k, SSM_STATE:, sl] = (b_r * lri + b_i * lrr).astype(BF16)
    g_re = jnp.concatenate(g_re, axis=1)
    g_im = jnp.concatenate(g_im, axis=1)
    kvec = (jnp.dot(bbr_r, g_re, preferred_element_type=F32, precision=lax.Precision.HIGHEST)
            + jnp.dot(bbi_r, g_im, preferred_element_type=F32, precision=lax.Precision.HIGHEST))
    klane = lax.broadcasted_iota(jnp.int32, kvec.shape, 1)
    krow = lax.broadcasted_iota(jnp.int32, kvec.shape, 0)
    kvec = kvec + jnp.where(klane == krow * SSM_BLOCK, dcol_ref[k], 0.0)

    t_in = lax.broadcasted_iota(jnp.int32, (SSM_BLOCK, LANES), 0)
    causal = (lax.broadcasted_iota(jnp.int32, (SSM_BLOCK, LANES), 1) & (SSM_BLOCK - 1)) >= t_in
    for c in range(SSM_GROUP):
        for j in range(SSM_GROUP // 2):
            src = jnp.broadcast_to(kvec[c:c + 1, j * LANES:(j + 1) * LANES], (SSM_BLOCK, LANES))
            shifted = pltpu.roll(src, 0, 1, stride=1, stride_axis=0)
            t_ref[k, c * SSM_BLOCK:(c + 1) * SSM_BLOCK, j * LANES:(j + 1) * LANES] = (
                jnp.where(causal, shifted, 0.0).astype(BF16))
    a_re, a_im = lam_r
    for _ in range(n_bits):
        a_re, a_im = cmul(a_re, a_im, a_re, a_im)
    are_ref[k] = a_re
    aim_ref[k] = a_im


def _ssm_prep_inputs(log_dt, a_re, a_im, b_re, b_im, c_re, c_im, d):
    g, p, cg = SSM_GROUPS, SSM_STATE, SSM_GROUP
    params = jnp.stack([a_re, a_im, jnp.broadcast_to(log_dt[:, None], (g, p))], axis=-1)
    bcol = jnp.concatenate([b_re, b_im], axis=-1)
    ccol = jnp.concatenate([c_re.transpose(0, 2, 1), c_im.transpose(0, 2, 1)], axis=-1)
    return params, params.transpose(0, 2, 1), bcol, bcol.transpose(0, 2, 1), ccol, d.reshape(g, cg, 1)


def _ssm_in_body(x_ref, w_ref, b_ref, *rest, tm, groups):
    prep_refs, o_ref = rest[:-1 - 5 * bool(groups)], rest[-1 - 5 * bool(groups)]
    xb = x_ref[...].reshape(tm, D_MODEL).astype(BF16)
    ut = lax.dot_general(w_ref[...], xb, (((1,), (1,)), ((), ())), preferred_element_type=F32)
    o_ref[...] = ut + b_ref[...]
    for k in range(groups):
        _ssm_prep_group(k, *prep_refs, *rest[-5:])


def _ssm_in(x, x_spec, n_tiles, w_t, b_col, tm, prep_inputs=()):
    g, p, cg, n = SSM_GROUPS, SSM_STATE, SSM_GROUP, SSM_BLOCK_DIM
    groups = g // n_tiles if prep_inputs else 0
    spec = lambda *shape: pl.BlockSpec((groups,) + shape, lambda i: (i, 0, 0))
    prep_in = [spec(p, 3), spec(3, p), spec(p, 2 * cg), spec(2 * cg, p), spec(p, 2 * cg), spec(cg, 1)]
    prep_out = [spec(n, n), spec(2 * p, n), spec(2 * p, n), spec(1, p), spec(1, p)]
    prep_shapes = ([jax.ShapeDtypeStruct((g, n, n), BF16)] + [jax.ShapeDtypeStruct((g, 2 * p, n), BF16)] * 2
                   + [jax.ShapeDtypeStruct((g, 1, p), F32)] * 2)
    return pl.pallas_call(
        functools.partial(_ssm_in_body, tm=tm, groups=groups),
        grid=(n_tiles,),
        in_specs=[x_spec, _const_spec((D_MODEL, D_MODEL)), _const_spec((D_MODEL, 1))] + (prep_in if groups else []),
        out_specs=[pl.BlockSpec((D_MODEL, tm), lambda i: (0, i))] + (prep_out if groups else []),
        out_shape=[jax.ShapeDtypeStruct((D_MODEL, n_tiles * tm), F32)] + (prep_shapes if groups else []),
        compiler_params=_params(1),
        name="ssm_in",
    )(x, w_t, b_col, *prep_inputs)


def _swap_sublanes_with_slabs(slabs):
    sub = lax.broadcasted_iota(jnp.int32, slabs[0].shape, 1)
    for k in range(3):
        step = 1 << k
        upper = (sub & step) != 0
        nxt = list(slabs)
        for j in range(SUBLANES):
            if j & step:
                continue
            a_lo, a_hi = slabs[j], slabs[j | step]
            nxt[j] = jnp.where(upper, pltpu.roll(a_hi, step, 1), a_lo)
            nxt[j | step] = jnp.where(upper, a_hi, pltpu.roll(a_lo, SUBLANES - step, 1))
        slabs = nxt
    return slabs


def _rows_to_planes(x8, n_rows):
    q_n = n_rows // SUBLANES
    slabs = [jnp.stack([x8[:, (SUBLANES * q + j) * LANES:(SUBLANES * q + j + 1) * LANES] for q in range(q_n)])
             for j in range(SUBLANES)]
    return [p.reshape(n_rows, LANES) for p in _swap_sublanes_with_slabs(slabs)]


def _planes_to_rows(planes, n_rows):
    q_n = n_rows // SUBLANES
    slabs = _swap_sublanes_with_slabs([p.reshape(q_n, SUBLANES, LANES) for p in planes])
    return jnp.concatenate([slabs[j][q] for q in range(q_n) for j in range(SUBLANES)], axis=1)


def _ssm_core_group(k, x_ref, xs_ref, s0re_ref, s0im_ref, t_ref, wt_ref, v_ref, are_ref, aim_ref,
                    z_ref, zs_ref, sre_ref, sim_ref, pre_ref, pim_ref, s_in, *, half_rows, n_seq):
    n_prompt = 2 * half_rows
    n_rows = s_in.shape[1]
    ch0 = SSM_GROUP * k
    lo = lax.broadcasted_iota(jnp.int32, (half_rows, LANES), 1) < SSM_BLOCK
    lo_s = lax.broadcasted_iota(jnp.int32, (n_seq, LANES), 1) < SSM_BLOCK
    pad = jnp.zeros((n_rows - n_prompt - n_seq, LANES), F32)
    planes = []
    for h in range(SSM_GROUP // SUBLANES):
        planes += _rows_to_planes(x_ref[ch0 + SUBLANES * h:ch0 + SUBLANES * (h + 1), :], half_rows)
    cols = []
    z_planes = [None] * SSM_GROUP
    for j in range(SSM_GROUP // 2):
        xa = planes[2 * j]
        xb = planes[2 * j + 1]
        first = jnp.where(lo, xa, pltpu.roll(xb, SSM_BLOCK, 1))
        second = jnp.where(lo, pltpu.roll(xa, SSM_BLOCK, 1), xb)
        samp = jnp.where(lo_s, xs_ref[ch0 + 2 * j], pltpu.roll(xs_ref[ch0 + 2 * j + 1], SSM_BLOCK, 1))
        cols.append(jnp.concatenate([first, second, samp, pad], axis=0).astype(BF16))
    u = jnp.concatenate(cols, axis=1)

    inj = lax.dot_general(u, wt_ref[k], (((1,), (1,)), ((), ())), preferred_element_type=F32)
    inj_re, inj_im = inj[:, :SSM_STATE], inj[:, SSM_STATE:]
    y_u = jnp.dot(u, t_ref[k], preferred_element_type=F32)
    a_re, a_im = are_ref[k], aim_ref[k]

    s_re = jnp.zeros((n_seq, SSM_STATE), F32)
    s_im = jnp.zeros((n_seq, SSM_STATE), F32)
    for n in range(n_prompt // n_seq):
        r = slice(n * n_seq, (n + 1) * n_seq)
        s_in[k, r, :SSM_STATE] = s_re
        s_in[k, r, SSM_STATE:] = s_im
        s_re, s_im = (a_re * s_re - a_im * s_im + inj_re[r], a_re * s_im + a_im * s_re + inj_im[r])
    pre_ref[k] = s_re
    pim_ref[k] = s_im
    rs = slice(n_prompt, n_prompt + n_seq)
    s0_re, s0_im = s0re_ref[k], s0im_ref[k]
    s_in[k, rs, :SSM_STATE] = s0_re
    s_in[k, rs, SSM_STATE:] = s0_im
    s_in[k, n_prompt + n_seq:, :] = jnp.zeros((n_rows - n_prompt - n_seq, 2 * SSM_STATE), F32)
    sre_ref[k] = a_re * s0_re - a_im * s0_im + inj_re[rs]
    sim_ref[k] = a_re * s0_im + a_im * s0_re + inj_im[rs]

    s_bf = s_in[k].astype(BF16)
    width = 2 * LANES
    for h in range(SSM_BLOCK_DIM // width):
        y = y_u[:, h * width:(h + 1) * width] + jnp.dot(s_bf, v_ref[k, :, h * width:(h + 1) * width],
                                                        preferred_element_type=F32)
        z = _gelu_tanh(y)
        for jj in range(width // LANES):
            j = h * (width // LANES) + jj
            blk = z[:, jj * LANES:(jj + 1) * LANES]
            first, second = blk[:half_rows], blk[half_rows:n_prompt]
            z_planes[2 * j] = jnp.where(lo, first, pltpu.roll(second, SSM_BLOCK, 1))
            z_planes[2 * j + 1] = jnp.where(lo, pltpu.roll(first, SSM_BLOCK, 1), second)
            samp = blk[rs]
            zs_ref[ch0 + 2 * j] = samp
            zs_ref[ch0 + 2 * j + 1] = pltpu.roll(samp, SSM_BLOCK, 1)
    z_ref[ch0:ch0 + SSM_GROUP, :] = jnp.concatenate(
        [_planes_to_rows(z_planes[SUBLANES * h:SUBLANES * (h + 1)], half_rows) for h in range(SSM_GROUP // SUBLANES)],
        axis=0).astype(z_ref.dtype)


def _ssm_core_body(*refs, half_rows, n_seq, groups):
    for k in range(groups):
        _ssm_core_group(k, *refs, half_rows=half_rows, n_seq=n_seq)


def _ssm_core(ut, uts3, s0_re, s0_im, ops, *, half_rows, n_seq):
    t_op, wt, v_op, a_re, a_im = ops
    g, p, cg, n = SSM_GROUPS, SSM_STATE, SSM_GROUP, SSM_BLOCK_DIM
    n_rows = 2 * half_rows + 2 * n_seq
    groups = SSM_GROUPS_PER_STEP
    grp = lambda *shape: pl.BlockSpec((groups,) + shape, lambda i: (i, 0, 0))
    chan = lambda r: pl.BlockSpec((groups * cg, r, LANES), lambda i: (i, 0, 0))
    rows = pl.BlockSpec((groups * cg, half_rows * LANES), lambda i: (i, 0))
    state = jax.ShapeDtypeStruct((g, n_seq, p), F32)
    return pl.pallas_call(
        functools.partial(_ssm_core_body, half_rows=half_rows, n_seq=n_seq, groups=groups),
        grid=(g // groups,),
        in_specs=[rows, chan(n_seq), grp(n_seq, p), grp(n_seq, p),
                  grp(n, n), grp(2 * p, n), grp(2 * p, n), grp(1, p), grp(1, p)],
        out_specs=[rows, chan(n_seq), grp(n_seq, p), grp(n_seq, p), grp(n_seq, p), grp(n_seq, p)],
        out_shape=[jax.ShapeDtypeStruct(ut.shape, BF16), jax.ShapeDtypeStruct(uts3.shape, F32),
                   state, state, state, state],
        scratch_shapes=[pltpu.VMEM((groups, n_rows, 2 * p), F32)],
        compiler_params=_params(1),
        name="ssm_core",
    )(ut, uts3, s0_re, s0_im, t_op, wt, v_op, a_re, a_im)


def _glu_body(x_ref, zt_ref, w_ref, b_ref, g_ref, bb_ref, o_ref, *, tm):
    chunk = 2 * LANES
    lead = chunk // (tm // x_ref.shape[0])

    def gate_values(h):
        zt = zt_ref[:, h * chunk:(h + 1) * chunk]
        return (jnp.dot(w_ref[:D_MODEL, :], zt, preferred_element_type=F32),
                jnp.dot(w_ref[D_MODEL:, :], zt, preferred_element_type=F32))

    gv_next = gate_values(0)
    for h in range(tm // chunk):
        value, gate = gv_next[0] + b_ref[:D_MODEL, :], gv_next[1] + b_ref[D_MODEL:, :]
        if h + 1 < tm // chunk:
            gv_next = gate_values(h + 1)
        mixed = value * _sigmoid(gate)
        x = x_ref[h * lead:(h + 1) * lead].reshape(chunk, D_MODEL)
        y = DN_ALPHA * x + mixed.T
        o_ref[h * lead:(h + 1) * lead] = _layer_norm(y, g_ref[...], bb_ref[...]).reshape((lead,) + o_ref.shape[1:])


def _glu_ln(x, x_spec, n_tiles, zt, w_t, b_col, gain, bias, tm):
    return pl.pallas_call(
        functools.partial(_glu_body, tm=tm),
        grid=(n_tiles,),
        in_specs=[x_spec, pl.BlockSpec((D_MODEL, tm), lambda i: (0, i)),
                  _const_spec((2 * D_MODEL, D_MODEL)), _const_spec((2 * D_MODEL, 1)),
                  _const_spec((1, D_MODEL)), _const_spec((1, D_MODEL))],
        out_specs=x_spec,
        out_shape=jax.ShapeDtypeStruct(x.shape, F32),
        compiler_params=_params(1),
        name="glu_ln",
    )(x, zt, w_t, b_col, gain, bias)


def kernel(x_prompt, x_sample, cache_k, cache_v, state_ssm_re, state_ssm_im, attn_w_qkv, attn_b_qkv, attn_sinks, attn_w_o, attn_b_o, ssm_w_in, ssm_b_in, ssm_log_dt, ssm_a_re, ssm_a_im, ssm_b_re, ssm_b_im, ssm_c_re, ssm_c_im, ssm_d, ssm_w_glu, ssm_b_glu, ffn_w_up, ffn_w_down, ln_gain, ln_bias):
    n_b, seq, _ = x_prompt.shape
    n_s, seq_s, _ = x_sample.shape
    n_p, n_d = n_b * seq, n_s * seq_s
    tm = 512
    blocks = seq // SSM_BLOCK
    half_rows = (blocks // 2) * n_b
    ssm_tm = 2 * SSM_BLOCK * n_b

    xp = x_prompt.reshape(n_p, D_MODEL)
    xs = x_sample.reshape(n_d, D_MODEL)
    tabs_p = _rope_tables(np.arange(seq))
    tabs_s = _rope_tables(np.tile(PAST_LEN + np.arange(seq_s), n_s))

    ssm_view = (n_b, 2, blocks // 2, SSM_BLOCK, D_MODEL)
    ssm_spec = pl.BlockSpec((n_b, 2, 1, SSM_BLOCK, D_MODEL), lambda i: (0, 0, i, 0, 0))
    flat_spec = pl.BlockSpec((n_d, D_MODEL), lambda i: (i, 0))

    w_up_all = ffn_w_up.astype(BF16)
    w_dn_all = ffn_w_down.astype(BF16)
    new_k_p, new_v_p, new_re_p, new_im_p = [], [], [], []
    new_k_s, new_v_s, new_re_s, new_im_s = [], [], [], []
    for i in range(DEPTH):
        l = i // 2
        gain = ln_gain[i][:, None, :]
        bias = ln_bias[i][:, None, :]
        if i % 2 == 0:
            w_qkv_t = attn_w_qkv[l].T.astype(BF16)
            b_qkv = attn_b_qkv[l][:, None]
            w_o_t = attn_w_o[l].T.astype(BF16)
            b_o = attn_b_o[l][None]
            sink_rows = jnp.repeat(attn_sinks[l].astype(F32), PAIR).reshape(N_KV_HEADS, 1, GQA_GROUP * PAIR)
            qp, kp, vp = _qkv_rope(xp, w_qkv_t, b_qkv, tabs_p, 2 * tm, seq // (2 * tm))
            qs, ks, vs = _qkv_rope(xs, w_qkv_t, b_qkv, tabs_s, n_d, 1)
            prev_p = lambda b, t: (0, jnp.maximum(b * (seq // WINDOW) + t * (2 * tm // WINDOW) - 1, 0))
            xp = _attn_proj_ln(xp, qp, kp, kp, vp, vp, sink_rows, w_o_t, b_o, gain[0], bias[0],
                               n_batch=n_b, tq=2 * tm, prev_map=prev_p, has_start=True)
            pad_rows = lambda a: jnp.pad(a.reshape(n_s, seq_s, -1), ((0, 0), (0, PAIR - seq_s), (0, 0))).reshape(n_s * PAIR, -1)
            pad_cols = lambda a: jnp.pad(a.reshape(-1, n_s, seq_s), ((0, 0), (0, 0), (0, PAIR - seq_s))).reshape(-1, n_s * PAIR)
            ck_t = cache_k[l].reshape(n_s * WINDOW, KV_DIM).T
            cv_t = cache_v[l].reshape(n_s * WINDOW, KV_DIM).T
            xs = _attn_proj_ln(pad_rows(xs), pad_cols(qs), ck_t, pad_cols(ks), cv_t, pad_cols(vs), sink_rows, w_o_t, b_o,
                               gain[0], bias[0], n_batch=n_s, tq=PAIR, prev_map=lambda b, t: (0, b), has_start=False)
            xs = xs.reshape(n_s, PAIR, D_MODEL)[:, :seq_s].reshape(n_d, D_MODEL)
            last = lambda a: (a.reshape(KV_DIM, n_b, seq)[:, :, seq - WINDOW:].transpose(1, 2, 0)
                              .reshape(n_b, WINDOW, N_KV_HEADS, HEAD_DIM))
            new_k_p.append(last(kp))
            new_v_p.append(last(vp))
            ks4 = ks.T.reshape(n_s, seq_s, N_KV_HEADS, HEAD_DIM)
            vs4 = vs.T.reshape(n_s, seq_s, N_KV_HEADS, HEAD_DIM)
            new_k_s.append(jnp.concatenate([cache_k[l], ks4], axis=1)[:, -WINDOW:])
            new_v_s.append(jnp.concatenate([cache_v[l], vs4], axis=1)[:, -WINDOW:])
        else:
            w_in_t = ssm_w_in[l].T.astype(BF16)
            b_in = ssm_b_in[l][:, None]
            w_glu_t = ssm_w_glu[l].T.astype(BF16)
            b_glu = ssm_b_glu[l][:, None]
            prep_inputs = _ssm_prep_inputs(ssm_log_dt[l], ssm_a_re[l], ssm_a_im[l], ssm_b_re[l], ssm_b_im[l],
                                           ssm_c_re[l], ssm_c_im[l], ssm_d[l])
            ut, *ops = _ssm_in(xp.reshape(ssm_view), ssm_spec, blocks // 2, w_in_t, b_in, ssm_tm, prep_inputs)
            uts, = _ssm_in(xs, flat_spec, 1, w_in_t, b_in, n_d)
            uts3 = jnp.pad(uts.reshape(D_MODEL, n_s, seq_s), ((0, 0), (0, 0), (0, LANES - seq_s)))
            s0_re = state_ssm_re[l].transpose(1, 0, 2)
            s0_im = state_ssm_im[l].transpose(1, 0, 2)
            zt, zs3, s_re, s_im, p_re, p_im = _ssm_core(ut, uts3, s0_re, s0_im, ops, half_rows=half_rows, n_seq=n_s)
            zst = zs3[:, :, :seq_s].reshape(D_MODEL, n_d).astype(BF16)
            xp = _glu_ln(xp.reshape(ssm_view), ssm_spec, blocks // 2, zt, w_glu_t, b_glu, gain[0], bias[0],
                         ssm_tm).reshape(n_p, D_MODEL)
            xs = _glu_ln(xs, flat_spec, 1, zst, w_glu_t, b_glu, gain[0], bias[0], n_d)
            new_re_p.append(p_re.transpose(1, 0, 2))
            new_im_p.append(p_im.transpose(1, 0, 2))
            new_re_s.append(s_re.transpose(1, 0, 2))
            new_im_s.append(s_im.transpose(1, 0, 2))
        xp = _ffn_ln(xp, w_up_all, w_dn_all, i, gain[1], bias[1], 2 * tm)
        xs = _ffn_ln(xs, w_up_all, w_dn_all, i, gain[1], bias[1], n_d)
    return (xp.reshape(x_prompt.shape), xs.reshape(x_sample.shape),
            jnp.stack(new_k_p), jnp.stack(new_v_p), jnp.stack(new_re_p), jnp.stack(new_im_p),
            jnp.stack(new_k_s), jnp.stack(new_v_s), jnp.stack(new_re_s), jnp.stack(new_im_s))
```

```python
import functools

import jax
import jax.numpy as jnp
import numpy as np
from jax import lax
from jax.experimental import pallas as pl
from jax.experimental.pallas import tpu as pltpu

F32 = jnp.float32
BF16 = jnp.bfloat16

D_MODEL = 1024
DEPTH = 4
CHUNK = 64
N_HEADS = 16
N_KV_HEADS = 2
HEAD_DIM = 64
GQA_GROUP = N_HEADS // N_KV_HEADS
Q_DIM = N_HEADS * HEAD_DIM
KV_DIM = N_KV_HEADS * HEAD_DIM
QKV_DIM = Q_DIM + 2 * KV_DIM
WINDOW = 128
PAST_LEN = 4096
ROPE_THETA = 10000.0
SSM_GROUP = 16
SSM_GROUPS = D_MODEL // SSM_GROUP
SSM_STATE = 64
D_FF = 2816
DN_ALPHA = (2.0 * DEPTH) ** 0.25
LN_EPS = 1e-5
NEG_INF = -1e30
LOG2E = 1.4426950408889634

LANES = 128
SUBLANES = 8
BF16_SUBLANES = 16
FF_CHUNK = 256
N_FF_CHUNKS = D_FF // FF_CHUNK
SSM_BLOCK = CHUNK
SSM_BLOCK_DIM = SSM_GROUP * SSM_BLOCK
SSM_GROUPS_PER_STEP = 2
VMEM_LIMIT = 56 * 2 ** 20


def _params(n_axes):
    return pltpu.CompilerParams(dimension_semantics=("arbitrary",) * n_axes, vmem_limit_bytes=VMEM_LIMIT)


def _const_spec(shape):
    zeros = (0,) * len(shape)
    return pl.BlockSpec(shape, lambda *_: zeros, pipeline_mode=pl.Buffered(1))


def _sigmoid(x):
    return 0.5 * jnp.tanh(0.5 * x) + 0.5


def _gelu_tanh(x):
    k = (2.0 / np.pi) ** 0.5
    half_x = 0.5 * x
    return half_x * jnp.tanh(x * (k + (k * 0.044715) * (x * x))) + half_x


def _layer_norm(y, gain, bias):
    mu = jnp.mean(y, axis=-1, keepdims=True)
    d = y - mu
    var = jnp.mean(d * d, axis=-1, keepdims=True)
    return d * lax.rsqrt(var + LN_EPS) * gain + bias


def _ffn_body(x_ref, wup_ref, wdn_ref, g_ref, b_ref, o_ref):
    x = x_ref[...]
    xb = x.astype(BF16)
    acc = DN_ALPHA * x
    for c in range(N_FF_CHUNKS):
        cols = slice(FF_CHUNK * c, FF_CHUNK * (c + 1))
        gate = jnp.dot(xb, wup_ref[:, cols], preferred_element_type=F32)
        up = jnp.dot(xb, wup_ref[:, D_FF + FF_CHUNK * c:D_FF + FF_CHUNK * (c + 1)], preferred_element_type=F32)
        act = gate * _sigmoid(gate) * up
        acc = acc + jnp.dot(act.astype(BF16), wdn_ref[cols, :], preferred_element_type=F32)
    o_ref[...] = _layer_norm(acc, g_ref[...], b_ref[...])


def _ffn_ln(x, wup_all, wdn_all, layer, gain, bias, tm):
    n = x.shape[0]
    layer_spec = lambda *shape: pl.BlockSpec((None,) + shape, lambda i: (layer, 0, 0), pipeline_mode=pl.Buffered(1))
    return pl.pallas_call(
        _ffn_body,
        grid=(n // tm,),
        in_specs=[pl.BlockSpec((tm, D_MODEL), lambda i: (i, 0)),
                  layer_spec(D_MODEL, 2 * D_FF), layer_spec(D_FF, D_MODEL),
                  _const_spec((1, D_MODEL)), _const_spec((1, D_MODEL))],
        out_specs=pl.BlockSpec((tm, D_MODEL), lambda i: (i, 0)),
        out_shape=jax.ShapeDtypeStruct((n, D_MODEL), F32),
        compiler_params=_params(1),
        name="ffn_ln",
    )(x, wup_all, wdn_all, gain, bias)


def _qkv_body(x_ref, w_ref, b_ref, cos_ref, sin_ref, q_ref, k_ref, v_ref):
    half = HEAD_DIM // 2
    chunk = 2 * LANES
    n_chunks = x_ref.shape[0] // chunk
    nt = (((1,), (1,)), ((), ()))

    def project(h):
        xb = x_ref[h * chunk:(h + 1) * chunk, :].astype(BF16)
        return (lax.dot_general(w_ref[:Q_DIM, :], xb, nt, preferred_element_type=F32),
                lax.dot_general(w_ref[Q_DIM:, :], xb, nt, preferred_element_type=F32))

    def rope(t, cos, sin):
        rot = t.reshape(t.shape[0] // HEAD_DIM, 2, half, chunk)
        first, second = rot[:, 0], rot[:, 1]
        return jnp.stack([first * cos - second * sin, second * cos + first * sin], axis=1).reshape(t.shape)

    nxt = project(0)
    for h in range(n_chunks):
        cols = slice(h * chunk, (h + 1) * chunk)
        q_t, kv_t = nxt[0] + b_ref[:Q_DIM, :], nxt[1] + b_ref[Q_DIM:, :]
        if h + 1 < n_chunks:
            nxt = project(h + 1)
        cos, sin = cos_ref[:, cols], sin_ref[:, cols]
        q_ref[:, cols] = (rope(q_t, cos, sin) * (HEAD_DIM ** -0.5 * LOG2E)).astype(BF16)
        k_ref[:, cols] = rope(kv_t[:KV_DIM], cos, sin)
        v_ref[:, cols] = kv_t[KV_DIM:]


def _qkv_rope(x, w_t, b_col, tabs, tm, tab_tiles):
    n = x.shape[0]
    tab_spec = pl.BlockSpec((HEAD_DIM // 2, tm), lambda i: (0, i % tab_tiles))
    col = lambda rows: pl.BlockSpec((rows, tm), lambda i: (0, i))
    return pl.pallas_call(
        _qkv_body,
        grid=(n // tm,),
        in_specs=[pl.BlockSpec((tm, D_MODEL), lambda i: (i, 0)),
                  _const_spec((QKV_DIM, D_MODEL)), _const_spec((QKV_DIM, 1)), tab_spec, tab_spec],
        out_specs=[col(Q_DIM), col(KV_DIM), col(KV_DIM)],
        out_shape=[jax.ShapeDtypeStruct((Q_DIM, n), BF16),
                   jax.ShapeDtypeStruct((KV_DIM, n), F32),
                   jax.ShapeDtypeStruct((KV_DIM, n), F32)],
        compiler_params=_params(1),
        name="qkv_rope",
    )(x, w_t, b_col, *tabs)


def _rope_tables(pos):
    half = HEAD_DIM // 2
    inv = ROPE_THETA ** (-np.arange(half, dtype=np.float64) / half)
    ang = inv[:, None] * np.asarray(pos, np.float64)[None, :]
    return tuple(jnp.asarray(t.astype(np.float32)) for t in (np.cos(ang), np.sin(ang)))


PAIR = 2 * CHUNK


def _attn_body(x_ref, q_ref, kp_ref, kc_ref, vp_ref, vc_ref, sink_ref, wot_ref, bo_ref, g_ref, b_ref,
               out_ref, ot_scr, *, tq, has_start):
    span = WINDOW + PAIR
    k_t = jnp.concatenate([kp_ref[...], kc_ref[...]], axis=1)
    v_t = jnp.concatenate([vp_ref[...], vc_ref[...]], axis=1).astype(BF16)
    k_rows = [k_t[m * HEAD_DIM:(m + 1) * HEAD_DIM].T.astype(BF16) for m in range(N_KV_HEADS)]
    ones_rows = jnp.ones((BF16_SUBLANES, v_t.shape[1]), BF16)
    v_aug = [jnp.concatenate([v_t[m * HEAD_DIM:(m + 1) * HEAD_DIM], ones_rows], axis=0) for m in range(N_KV_HEADS)]
    n_cols = GQA_GROUP * PAIR
    tok = lax.broadcasted_iota(jnp.int32, (CHUNK, n_cols), 1) & (PAIR - 1)
    first_chunk = tok < CHUNK
    if has_start:
        after_start = jnp.broadcast_to(pl.program_id(1) > 0, first_chunk.shape)

    def scores(pr, m):
        heads = range(GQA_GROUP * m, GQA_GROUP * (m + 1))
        q_m = jnp.concatenate([q_ref[h * HEAD_DIM:(h + 1) * HEAD_DIM, pr * PAIR:(pr + 1) * PAIR] for h in heads], axis=1)
        return jnp.dot(k_rows[m][pr * PAIR:pr * PAIR + span], q_m, preferred_element_type=F32)

    steps = [(pr, m) for pr in range(tq // PAIR) for m in range(N_KV_HEADS)]
    proj_pairs = min(2, tq // PAIR)
    s_next = scores(*steps[0])
    for idx, (pr, m) in enumerate(steps):
        s = s_next
        if idx + 1 < len(steps):
            s_next = scores(*steps[idx + 1])
        rows = slice(pr * PAIR, (pr + 1) * PAIR)
        oldest, older = s[:CHUNK], s[CHUNK:2 * CHUNK]
        if has_start and pr == 0:
            oldest = jnp.where(after_start, oldest, NEG_INF)
            older = jnp.where(after_start, older, NEG_INF)
        edge = jnp.where(first_chunk, oldest, s[3 * CHUNK:])
        s = jnp.concatenate([edge, older, s[2 * CHUNK:3 * CHUNK]], axis=0)
        sink = sink_ref[m] * LOG2E
        mx = jnp.maximum(jnp.max(s, axis=0, keepdims=True), sink)
        e = jnp.exp2(s - mx)
        e_edge = e[:CHUNK]
        p_t = jnp.concatenate([jnp.where(first_chunk, e_edge, 0.0), e[CHUNK:],
                               jnp.where(first_chunk, 0.0, e_edge)], axis=0).astype(BF16)
        o_aug = jnp.dot(v_aug[m][:, pr * PAIR:pr * PAIR + span], p_t, preferred_element_type=F32)
        den = o_aug[HEAD_DIM:HEAD_DIM + 1] + jnp.exp2(sink - mx)
        o_t = o_aug[:HEAD_DIM] * (1.0 / den)
        for hb in range(GQA_GROUP):
            head = GQA_GROUP * m + hb
            ot_scr[head * HEAD_DIM:(head + 1) * HEAD_DIM, rows] = o_t[:, hb * PAIR:(hb + 1) * PAIR].astype(BF16)
        if m == N_KV_HEADS - 1 and (pr + 1) % proj_pairs == 0:
            blk = slice((pr + 1 - proj_pairs) * PAIR, (pr + 1) * PAIR)
            proj_t = jnp.dot(wot_ref[...], ot_scr[:, blk], preferred_element_type=F32)
            y = DN_ALPHA * x_ref[blk, :] + proj_t.T + bo_ref[...]
            out_ref[blk, :] = _layer_norm(y, g_ref[...], b_ref[...])


def _attn_proj_ln(x, q_t, k_prev, k_cur, v_prev, v_cur, sink_rows, wo_t, bo, gain, bias, *, n_batch, tq, prev_map, has_start):
    n = x.shape[0]
    tiles = n // (n_batch * tq)
    cur = lambda b, i: (b * tiles + i, 0)
    cur_t = lambda b, i: (0, b * tiles + i)
    return pl.pallas_call(
        functools.partial(_attn_body, tq=tq, has_start=has_start),
        grid=(n_batch, tiles),
        in_specs=[pl.BlockSpec((tq, D_MODEL), cur), pl.BlockSpec((Q_DIM, tq), cur_t),
                  pl.BlockSpec((KV_DIM, WINDOW), prev_map), pl.BlockSpec((KV_DIM, tq), cur_t),
                  pl.BlockSpec((KV_DIM, WINDOW), prev_map), pl.BlockSpec((KV_DIM, tq), cur_t),
                  _const_spec((N_KV_HEADS, 1, GQA_GROUP * PAIR)),
                  _const_spec((D_MODEL, Q_DIM)), _const_spec((1, D_MODEL)),
                  _const_spec((1, D_MODEL)), _const_spec((1, D_MODEL))],
        out_specs=pl.BlockSpec((tq, D_MODEL), cur),
        out_shape=jax.ShapeDtypeStruct((n, D_MODEL), F32),
        scratch_shapes=[pltpu.VMEM((Q_DIM, tq), BF16)],
        compiler_params=_params(2),
        name="attn_proj_ln",
    )(x, q_t, k_prev, k_cur, v_prev, v_cur, sink_rows, wo_t, bo, gain, bias)


def _ssm_prep_group(k, pcol_ref, prow_ref, bcol_ref, brow_ref, ccol_ref, dcol_ref, t_ref, wt_ref, v_ref, are_ref, aim_ref):
    def cmul(x_re, x_im, y_re, y_im):
        return x_re * y_re - x_im * y_im, x_re * y_im + x_im * y_re

    def a_bar(a_re, a_im, dt):
        mag = jnp.exp(dt * a_re)
        ang = dt * a_im
        return mag * jnp.cos(ang), mag * jnp.sin(ang)

    def zoh_factor(l_re, l_im, a_re, a_im):
        den = a_re * a_re + a_im * a_im
        n_re = l_re - 1.0
        return (n_re * a_re + l_im * a_im) / den, (l_im * a_re - n_re * a_im) / den

    n_bits = SSM_BLOCK.bit_length() - 1
    pc = pcol_ref[k]
    ar_c, ai_c, dt_c = pc[:, 0:1], pc[:, 1:2], jnp.exp(pc[:, 2:3])
    lam_c = a_bar(ar_c, ai_c, dt_c)
    fr_c, fi_c = zoh_factor(*lam_c, ar_c, ai_c)
    bc = bcol_ref[k]
    bbr_c = fr_c * bc[:, :SSM_GROUP] - fi_c * bc[:, SSM_GROUP:]
    bbi_c = fr_c * bc[:, SSM_GROUP:] + fi_c * bc[:, :SSM_GROUP]
    pr = prow_ref[k]
    ar_r, ai_r, dt_r = pr[0:1], pr[1:2], jnp.exp(pr[2:3])
    lam_r = a_bar(ar_r, ai_r, dt_r)
    fr_r, fi_r = zoh_factor(*lam_r, ar_r, ai_r)
    br = brow_ref[k]
    bbr_r = fr_r * br[:SSM_GROUP] - fi_r * br[SSM_GROUP:]
    bbi_r = fr_r * br[SSM_GROUP:] + fi_r * br[:SSM_GROUP]
    cc = ccol_ref[k]

    squares = [lam_c]
    for _ in range(n_bits - 1):
        squares.append(cmul(*squares[-1], *squares[-1]))

    def lane_pow(exps):
        acc = None
        for k, (q_re, q_im) in enumerate(squares):
            bit = ((exps >> k) & 1) == 1
            f_re, f_im = jnp.where(bit, q_re, 1.0), jnp.where(bit, q_im, 0.0)
            acc = (f_re, f_im) if acc is None else cmul(*acc, f_re, f_im)
        return acc

    lane = lax.broadcasted_iota(jnp.int32, (SSM_STATE, LANES), 1)
    lo = lane < SSM_BLOCK
    tau = lane & (SSM_BLOCK - 1)
    l0r, l0i = lane_pow(tau)
    l1r, l1i = cmul(l0r, l0i, *lam_c)
    lrr, lri = lane_pow(SSM_BLOCK - 1 - tau)

    g_re, g_im = [], []
    for j in range(SSM_GROUP // 2):
        sl = slice(j * LANES, (j + 1) * LANES)
        cr = jnp.where(lo, cc[:, 2 * j:2 * j + 1], cc[:, 2 * j + 1:2 * j + 2])
        ci = jnp.where(lo, cc[:, SSM_GROUP + 2 * j:SSM_GROUP + 2 * j + 1],
                       cc[:, SSM_GROUP + 2 * j + 1:SSM_GROUP + 2 * j + 2])
        g_re.append(cr * l0r - ci * l0i)
        g_im.append(-(cr * l0i + ci * l0r))
        v_ref[k, :SSM_STATE, sl] = (cr * l1r - ci * l1i).astype(BF16)
        v_ref[k, SSM_STATE:, sl] = (-(cr * l1i + ci * l1r)).astype(BF16)
        b_r = jnp.where(lo, bbr_c[:, 2 * j:2 * j + 1], bbr_c[:, 2 * j + 1:2 * j + 2])
        b_i = jnp.where(lo, bbi_c[:, 2 * j:2 * j + 1], bbi_c[:, 2 * j + 1:2 * j + 2])
        wt_ref[k, :SSM_STATE, sl] = (b_r * lrr - b_i * lri).astype(BF16)
        wt_ref[k, SSM_STATE:, sl] = (b_r * lri + b_i * lrr).astype(BF16)
    g_re = jnp.concatenate(g_re, axis=1)
    g_im = jnp.concatenate(g_im, axis=1)
    kvec = (jnp.dot(bbr_r, g_re, preferred_element_type=F32, precision=lax.Precision.HIGHEST)
            + jnp.dot(bbi_r, g_im, preferred_element_type=F32, precision=lax.Precision.HIGHEST))
    klane = lax.broadcasted_iota(jnp.int32, kvec.shape, 1)
    krow = lax.broadcasted_iota(jnp.int32, kvec.shape, 0)
    kvec = kvec + jnp.where(klane == krow * SSM_BLOCK, dcol_ref[k], 0.0)

    t_in = lax.broadcasted_iota(jnp.int32, (SSM_BLOCK, LANES), 0)
    causal = (lax.broadcasted_iota(jnp.int32, (SSM_BLOCK, LANES), 1) & (SSM_BLOCK - 1)) >= t_in
    for c in range(SSM_GROUP):
        for j in range(SSM_GROUP // 2):
            src = jnp.broadcast_to(kvec[c:c + 1, j * LANES:(j + 1) * LANES], (SSM_BLOCK, LANES))
            shifted = pltpu.roll(src, 0, 1, stride=1, stride_axis=0)
            t_ref[k, c * SSM_BLOCK:(c + 1) * SSM_BLOCK, j * LANES:(j + 1) * LANES] = (
                jnp.where(causal, shifted, 0.0).astype(BF16))
    a_re, a_im = lam_r
    for _ in range(n_bits):
        a_re, a_im = cmul(a_re, a_im, a_re, a_im)
    are_ref[k] = a_re
    aim_ref[k] = a_im


def _ssm_prep_inputs(log_dt, a_re, a_im, b_re, b_im, c_re, c_im, d):
    g, p, cg = SSM_GROUPS, SSM_STATE, SSM_GROUP
    params = jnp.stack([a_re, a_im, jnp.broadcast_to(log_dt[:, None], (g, p))], axis=-1)
    bcol = jnp.concatenate([b_re, b_im], axis=-1)
    ccol = jnp.concatenate([c_re.transpose(0, 2, 1), c_im.transpose(0, 2, 1)], axis=-1)
    return params, params.transpose(0, 2, 1), bcol, bcol.transpose(0, 2, 1), ccol, d.reshape(g, cg, 1)


def _ssm_in_body(x_ref, w_ref, b_ref, *rest, tm, groups):
    prep_refs, o_ref = rest[:-1 - 5 * bool(groups)], rest[-1 - 5 * bool(groups)]
    xb = x_ref[...].reshape(tm, D_MODEL).astype(BF16)
    ut = lax.dot_general(w_ref[...], xb, (((1,), (1,)), ((), ())), preferred_element_type=F32)
    ut = ut + b_ref[...]
    if len(o_ref.shape) == 2:
        o_ref[...] = ut
    else:
        for j in range(tm // LANES):
            o_ref[:, SUBLANES * j:SUBLANES * (j + 1), :] = (
                ut[:, j * LANES:(j + 1) * LANES].reshape(D_MODEL // SUBLANES, SUBLANES, LANES))
    for k in range(groups):
        _ssm_prep_group(k, *prep_refs, *rest[-5:])


def _ssm_in(x, x_spec, n_tiles, w_t, b_col, tm, prep_inputs=()):
    g, p, cg, n = SSM_GROUPS, SSM_STATE, SSM_GROUP, SSM_BLOCK_DIM
    groups = g // n_tiles if prep_inputs else 0
    spec = lambda *shape: pl.BlockSpec((groups,) + shape, lambda i: (i, 0, 0))
    prep_in = [spec(p, 3), spec(3, p), spec(p, 2 * cg), spec(2 * cg, p), spec(p, 2 * cg), spec(cg, 1)]
    prep_out = [spec(n, n), spec(2 * p, n), spec(2 * p, n), spec(1, p), spec(1, p)]
    prep_shapes = ([jax.ShapeDtypeStruct((g, n, n), BF16)] + [jax.ShapeDtypeStruct((g, 2 * p, n), BF16)] * 2
                   + [jax.ShapeDtypeStruct((g, 1, p), F32)] * 2)
    if groups:
        rows = SUBLANES * tm // LANES
        u_spec = pl.BlockSpec((D_MODEL // SUBLANES, rows, LANES), lambda i: (0, i, 0))
        u_shape = jax.ShapeDtypeStruct((D_MODEL // SUBLANES, n_tiles * rows, LANES), F32)
    else:
        u_spec = pl.BlockSpec((D_MODEL, tm), lambda i: (0, i))
        u_shape = jax.ShapeDtypeStruct((D_MODEL, n_tiles * tm), F32)
    return pl.pallas_call(
        functools.partial(_ssm_in_body, tm=tm, groups=groups),
        grid=(n_tiles,),
        in_specs=[x_spec, _const_spec((D_MODEL, D_MODEL)), _const_spec((D_MODEL, 1))] + (prep_in if groups else []),
        out_specs=[u_spec] + (prep_out if groups else []),
        out_shape=[u_shape] + (prep_shapes if groups else []),
        compiler_params=_params(1),
        name="ssm_in",
    )(x, w_t, b_col, *prep_inputs)


def _plane_rows(k, c, half_rows):
    octet = (SSM_GROUP // SUBLANES) * k + c // SUBLANES
    return pl.ds(octet * SUBLANES * half_rows + c % SUBLANES, half_rows, stride=SUBLANES)


def _ssm_core_group(k, x_ref, xs_ref, s0re_ref, s0im_ref, t_ref, wt_ref, v_ref, are_ref, aim_ref,
                    z_ref, zs_ref, sre_ref, sim_ref, pre_ref, pim_ref, s_in, *, half_rows, n_seq):
    n_prompt = 2 * half_rows
    n_rows = s_in.shape[1]
    ch0 = SSM_GROUP * k
    lo = lax.broadcasted_iota(jnp.int32, (half_rows, LANES), 1) < SSM_BLOCK
    lo_s = lax.broadcasted_iota(jnp.int32, (n_seq, LANES), 1) < SSM_BLOCK
    pad = jnp.zeros((n_rows - n_prompt - n_seq, LANES), F32)
    cols = []
    for j in range(SSM_GROUP // 2):
        xa = x_ref[_plane_rows(k, 2 * j, half_rows), :]
        xb = x_ref[_plane_rows(k, 2 * j + 1, half_rows), :]
        first = jnp.where(lo, xa, pltpu.roll(xb, SSM_BLOCK, 1))
        second = jnp.where(lo, pltpu.roll(xa, SSM_BLOCK, 1), xb)
        samp = jnp.where(lo_s, xs_ref[ch0 + 2 * j], pltpu.roll(xs_ref[ch0 + 2 * j + 1], SSM_BLOCK, 1))
        cols.append(jnp.concatenate([first, second, samp, pad], axis=0).astype(BF16))
    u = jnp.concatenate(cols, axis=1)

    inj = lax.dot_general(u, wt_ref[k], (((1,), (1,)), ((), ())), preferred_element_type=F32)
    inj_re, inj_im = inj[:, :SSM_STATE], inj[:, SSM_STATE:]
    y_u = jnp.dot(u, t_ref[k], preferred_element_type=F32)
    a_re, a_im = are_ref[k], aim_ref[k]

    s_re = jnp.zeros((n_seq, SSM_STATE), F32)
    s_im = jnp.zeros((n_seq, SSM_STATE), F32)
    for n in range(n_prompt // n_seq):
        r = slice(n * n_seq, (n + 1) * n_seq)
        s_in[k, r, :SSM_STATE] = s_re
        s_in[k, r, SSM_STATE:] = s_im
        s_re, s_im = (a_re * s_re - a_im * s_im + inj_re[r], a_re * s_im + a_im * s_re + inj_im[r])
    pre_ref[k] = s_re
    pim_ref[k] = s_im
    rs = slice(n_prompt, n_prompt + n_seq)
    s0_re, s0_im = s0re_ref[k], s0im_ref[k]
    s_in[k, rs, :SSM_STATE] = s0_re
    s_in[k, rs, SSM_STATE:] = s0_im
    s_in[k, n_prompt + n_seq:, :] = jnp.zeros((n_rows - n_prompt - n_seq, 2 * SSM_STATE), F32)
    sre_ref[k] = a_re * s0_re - a_im * s0_im + inj_re[rs]
    sim_ref[k] = a_re * s0_im + a_im * s0_re + inj_im[rs]

    s_bf = s_in[k].astype(BF16)
    width = 2 * LANES
    for h in range(SSM_BLOCK_DIM // width):
        y = y_u[:, h * width:(h + 1) * width] + jnp.dot(s_bf, v_ref[k, :, h * width:(h + 1) * width],
                                                        preferred_element_type=F32)
        z = _gelu_tanh(y)
        for jj in range(width // LANES):
            j = h * (width // LANES) + jj
            blk = z[:, jj * LANES:(jj + 1) * LANES]
            first, second = blk[:half_rows], blk[half_rows:n_prompt]
            z_ref[_plane_rows(k, 2 * j, half_rows), :] = jnp.where(lo, first, pltpu.roll(second, SSM_BLOCK, 1))
            z_ref[_plane_rows(k, 2 * j + 1, half_rows), :] = jnp.where(lo, pltpu.roll(first, SSM_BLOCK, 1), second)
            samp = blk[rs]
            zs_ref[ch0 + 2 * j] = samp
            zs_ref[ch0 + 2 * j + 1] = pltpu.roll(samp, SSM_BLOCK, 1)


def _ssm_core_body(*refs, half_rows, n_seq, groups):
    for k in range(groups):
        _ssm_core_group(k, *refs, half_rows=half_rows, n_seq=n_seq)


def _ssm_core(ut, uts3, s0_re, s0_im, ops, *, half_rows, n_seq):
    t_op, wt, v_op, a_re, a_im = ops
    g, p, cg, n = SSM_GROUPS, SSM_STATE, SSM_GROUP, SSM_BLOCK_DIM
    n_rows = 2 * half_rows + 2 * n_seq
    groups = SSM_GROUPS_PER_STEP
    grp = lambda *shape: pl.BlockSpec((groups,) + shape, lambda i: (i, 0, 0))
    chan = lambda r: pl.BlockSpec((groups * cg, r, LANES), lambda i: (i, 0, 0))
    rows = pl.BlockSpec((groups * cg * half_rows, LANES), lambda i: (i, 0))
    state = jax.ShapeDtypeStruct((g, n_seq, p), F32)
    return pl.pallas_call(
        functools.partial(_ssm_core_body, half_rows=half_rows, n_seq=n_seq, groups=groups),
        grid=(g // groups,),
        in_specs=[rows, chan(n_seq), grp(n_seq, p), grp(n_seq, p),
                  grp(n, n), grp(2 * p, n), grp(2 * p, n), grp(1, p), grp(1, p)],
        out_specs=[rows, chan(n_seq), grp(n_seq, p), grp(n_seq, p), grp(n_seq, p), grp(n_seq, p)],
        out_shape=[jax.ShapeDtypeStruct(ut.shape, F32), jax.ShapeDtypeStruct(uts3.shape, F32),
                   state, state, state, state],
        scratch_shapes=[pltpu.VMEM((groups, n_rows, 2 * p), F32)],
        compiler_params=_params(1),
        name="ssm_core",
    )(ut, uts3, s0_re, s0_im, t_op, wt, v_op, a_re, a_im)


def _glu_body(x_ref, zt_ref, w_ref, b_ref, g_ref, bb_ref, o_ref, *, tm):
    chunk = 2 * LANES
    lead = chunk // (tm // x_ref.shape[0])

    def gate_values(h):
        if len(zt_ref.shape) == 2:
            zt = zt_ref[:, h * chunk:(h + 1) * chunk]
        else:
            zt = jnp.concatenate(
                [zt_ref[:, SUBLANES * j:SUBLANES * (j + 1), :].reshape(D_MODEL, LANES)
                 for j in range(h * chunk // LANES, (h + 1) * chunk // LANES)], axis=1).astype(BF16)
        return (jnp.dot(w_ref[:D_MODEL, :], zt, preferred_element_type=F32),
                jnp.dot(w_ref[D_MODEL:, :], zt, preferred_element_type=F32))

    gv_next = gate_values(0)
    for h in range(tm // chunk):
        value, gate = gv_next[0] + b_ref[:D_MODEL, :], gv_next[1] + b_ref[D_MODEL:, :]
        if h + 1 < tm // chunk:
            gv_next = gate_values(h + 1)
        mixed = value * _sigmoid(gate)
        x = x_ref[h * lead:(h + 1) * lead].reshape(chunk, D_MODEL)
        y = DN_ALPHA * x + mixed.T
        o_ref[h * lead:(h + 1) * lead] = _layer_norm(y, g_ref[...], bb_ref[...]).reshape((lead,) + o_ref.shape[1:])


def _glu_ln(x, x_spec, n_tiles, zt, w_t, b_col, gain, bias, tm):
    if zt.ndim == 3:
        z_spec = pl.BlockSpec((D_MODEL // SUBLANES, SUBLANES * tm // LANES, LANES), lambda i: (0, i, 0))
    else:
        z_spec = pl.BlockSpec((D_MODEL, tm), lambda i: (0, i))
    return pl.pallas_call(
        functools.partial(_glu_body, tm=tm),
        grid=(n_tiles,),
        in_specs=[x_spec, z_spec,
                  _const_spec((2 * D_MODEL, D_MODEL)), _const_spec((2 * D_MODEL, 1)),
                  _const_spec((1, D_MODEL)), _const_spec((1, D_MODEL))],
        out_specs=x_spec,
        out_shape=jax.ShapeDtypeStruct(x.shape, F32),
        compiler_params=_params(1),
        name="glu_ln",
    )(x, zt, w_t, b_col, gain, bias)


def kernel(x_prompt, x_sample, cache_k, cache_v, state_ssm_re, state_ssm_im, attn_w_qkv, attn_b_qkv, attn_sinks, attn_w_o, attn_b_o, ssm_w_in, ssm_b_in, ssm_log_dt, ssm_a_re, ssm_a_im, ssm_b_re, ssm_b_im, ssm_c_re, ssm_c_im, ssm_d, ssm_w_glu, ssm_b_glu, ffn_w_up, ffn_w_down, ln_gain, ln_bias):
    n_b, seq, _ = x_prompt.shape
    n_s, seq_s, _ = x_sample.shape
    n_p, n_d = n_b * seq, n_s * seq_s
    tm = 512
    blocks = seq // SSM_BLOCK
    half_rows = (blocks // 2) * n_b
    ssm_tm = 2 * SSM_BLOCK * n_b

    xp = x_prompt.reshape(n_p, D_MODEL)
    xs = x_sample.reshape(n_d, D_MODEL)
    tabs_p = _rope_tables(np.arange(seq))
    tabs_s = _rope_tables(np.tile(PAST_LEN + np.arange(seq_s), n_s))

    ssm_view = (n_b, 2, blocks // 2, SSM_BLOCK, D_MODEL)
    ssm_spec = pl.BlockSpec((n_b, 2, 1, SSM_BLOCK, D_MODEL), lambda i: (0, 0, i, 0, 0))
    flat_spec = pl.BlockSpec((n_d, D_MODEL), lambda i: (i, 0))

    w_up_all = ffn_w_up.astype(BF16)
    w_dn_all = ffn_w_down.astype(BF16)
    new_k_p, new_v_p, new_re_p, new_im_p = [], [], [], []
    new_k_s, new_v_s, new_re_s, new_im_s = [], [], [], []
    for i in range(DEPTH):
        l = i // 2
        gain = ln_gain[i][:, None, :]
        bias = ln_bias[i][:, None, :]
        if i % 2 == 0:
            w_qkv_t = attn_w_qkv[l].T.astype(BF16)
            b_qkv = attn_b_qkv[l][:, None]
            w_o_t = attn_w_o[l].T.astype(BF16)
            b_o = attn_b_o[l][None]
            sink_rows = jnp.repeat(attn_sinks[l].astype(F32), PAIR).reshape(N_KV_HEADS, 1, GQA_GROUP * PAIR)
            qp, kp, vp = _qkv_rope(xp, w_qkv_t, b_qkv, tabs_p, 2 * tm, seq // (2 * tm))
            qs, ks, vs = _qkv_rope(xs, w_qkv_t, b_qkv, tabs_s, n_d, 1)
            prev_p = lambda b, t: (0, jnp.maximum(b * (seq // WINDOW) + t * (2 * tm // WINDOW) - 1, 0))
            xp = _attn_proj_ln(xp, qp, kp, kp, vp, vp, sink_rows, w_o_t, b_o, gain[0], bias[0],
                               n_batch=n_b, tq=2 * tm, prev_map=prev_p, has_start=True)
            pad_rows = lambda a: jnp.pad(a.reshape(n_s, seq_s, -1), ((0, 0), (0, PAIR - seq_s), (0, 0))).reshape(n_s * PAIR, -1)
            pad_cols = lambda a: jnp.pad(a.reshape(-1, n_s, seq_s), ((0, 0), (0, 0), (0, PAIR - seq_s))).reshape(-1, n_s * PAIR)
            ck_t = cache_k[l].reshape(n_s * WINDOW, KV_DIM).T
            cv_t = cache_v[l].reshape(n_s * WINDOW, KV_DIM).T
            xs = _attn_proj_ln(pad_rows(xs), pad_cols(qs), ck_t, pad_cols(ks), cv_t, pad_cols(vs), sink_rows, w_o_t, b_o,
                               gain[0], bias[0], n_batch=n_s, tq=PAIR, prev_map=lambda b, t: (0, b), has_start=False)
            xs = xs.reshape(n_s, PAIR, D_MODEL)[:, :seq_s].reshape(n_d, D_MODEL)
            last = lambda a: (jnp.stack([a[:, (b + 1) * seq - WINDOW:(b + 1) * seq] for b in range(n_b)])
                              .transpose(0, 2, 1).reshape(n_b, WINDOW, N_KV_HEADS, HEAD_DIM))
            new_k_p.append(last(kp))
            new_v_p.append(last(vp))
            ks4 = ks.T.reshape(n_s, seq_s, N_KV_HEADS, HEAD_DIM)
            vs4 = vs.T.reshape(n_s, seq_s, N_KV_HEADS, HEAD_DIM)
            new_k_s.append(jnp.concatenate([cache_k[l], ks4], axis=1)[:, -WINDOW:])
            new_v_s.append(jnp.concatenate([cache_v[l], vs4], axis=1)[:, -WINDOW:])
        else:
            w_in_t = ssm_w_in[l].T.astype(BF16)
            b_in = ssm_b_in[l][:, None]
            w_glu_t = ssm_w_glu[l].T.astype(BF16)
            b_glu = ssm_b_glu[l][:, None]
            prep_inputs = _ssm_prep_inputs(ssm_log_dt[l], ssm_a_re[l], ssm_a_im[l], ssm_b_re[l], ssm_b_im[l],
                                           ssm_c_re[l], ssm_c_im[l], ssm_d[l])
            ut, *ops = _ssm_in(xp.reshape(ssm_view), ssm_spec, blocks // 2, w_in_t, b_in, ssm_tm, prep_inputs)
            uts, = _ssm_in(xs, flat_spec, 1, w_in_t, b_in, n_d)
            uts3 = jnp.pad(uts.reshape(D_MODEL, n_s, seq_s), ((0, 0), (0, 0), (0, LANES - seq_s)))
            s0_re = state_ssm_re[l].transpose(1, 0, 2)
            s0_im = state_ssm_im[l].transpose(1, 0, 2)
            zt, zs3, s_re, s_im, p_re, p_im = _ssm_core(ut.reshape(-1, LANES), uts3, s0_re, s0_im, ops,
                                                        half_rows=half_rows, n_seq=n_s)
            zst = zs3[:, :, :seq_s].reshape(D_MODEL, n_d).astype(BF16)
            xp = _glu_ln(xp.reshape(ssm_view), ssm_spec, blocks // 2, zt.reshape(ut.shape), w_glu_t, b_glu,
                         gain[0], bias[0], ssm_tm).reshape(n_p, D_MODEL)
            xs = _glu_ln(xs, flat_spec, 1, zst, w_glu_t, b_glu, gain[0], bias[0], n_d)
            new_re_p.append(p_re.transpose(1, 0, 2))
            new_im_p.append(p_im.transpose(1, 0, 2))
            new_re_s.append(s_re.transpose(1, 0, 2))
            new_im_s.append(s_im.transpose(1, 0, 2))
        xp = _ffn_ln(xp, w_up_all, w_dn_all, i, gain[1], bias[1], 2 * tm)
        xs = _ffn_ln(xs, w_up_all, w_dn_all, i, gain[1], bias[1], n_d)
    return (xp.reshape(x_prompt.shape), xs.reshape(x_sample.shape),
            jnp.stack(new_k_p), jnp.stack(new_v_p), jnp.stack(new_re_p), jnp.stack(new_im_p),
            jnp.stack(new_k_s), jnp.stack(new_v_s), jnp.stack(new_re_s), jnp.stack(new_im_s))
```

```python
import functools

import jax
import jax.numpy as jnp
import numpy as np
from jax import lax
from jax.experimental import pallas as pl
from jax.experimental.pallas import tpu as pltpu

F32 = jnp.float32
BF16 = jnp.bfloat16

D_MODEL = 1024
DEPTH = 4
CHUNK = 64
N_HEADS = 16
N_KV_HEADS = 2
HEAD_DIM = 64
GQA_GROUP = N_HEADS // N_KV_HEADS
Q_DIM = N_HEADS * HEAD_DIM
KV_DIM = N_KV_HEADS * HEAD_DIM
QKV_DIM = Q_DIM + 2 * KV_DIM
WINDOW = 128
PAST_LEN = 4096
ROPE_THETA = 10000.0
SSM_GROUP = 16
SSM_GROUPS = D_MODEL // SSM_GROUP
SSM_STATE = 64
D_FF = 2816
DN_ALPHA = (2.0 * DEPTH) ** 0.25
LN_EPS = 1e-5
NEG_INF = -1e30
LOG2E = 1.4426950408889634

LANES = 128
SUBLANES = 8
BF16_SUBLANES = 16
FF_CHUNK = 256
N_FF_CHUNKS = D_FF // FF_CHUNK
SSM_BLOCK = CHUNK
SSM_BLOCK_DIM = SSM_GROUP * SSM_BLOCK
SSM_GROUPS_PER_STEP = 2
FFN_TILE = 1024
QKV_TILE = 2048
ATTN_TILE = 2048
VMEM_LIMIT = 56 * 2 ** 20


def _params(n_axes):
    return pltpu.CompilerParams(dimension_semantics=("arbitrary",) * n_axes, vmem_limit_bytes=VMEM_LIMIT)


def _const_spec(shape):
    zeros = (0,) * len(shape)
    return pl.BlockSpec(shape, lambda *_: zeros, pipeline_mode=pl.Buffered(1))


def _sigmoid(x):
    return 0.5 * jnp.tanh(0.5 * x) + 0.5


def _gelu_tanh(x):
    k = (2.0 / np.pi) ** 0.5
    half_x = 0.5 * x
    return half_x * jnp.tanh(x * (k + (k * 0.044715) * (x * x))) + half_x


def _layer_norm(y, gain, bias):
    mu = jnp.mean(y, axis=-1, keepdims=True)
    d = y - mu
    var = jnp.mean(d * d, axis=-1, keepdims=True)
    return d * lax.rsqrt(var + LN_EPS) * gain + bias


def _ffn_body(x_ref, wup_ref, wdn_ref, g_ref, b_ref, o_ref):
    x = x_ref[...]
    xb = x.astype(BF16)
    acc = DN_ALPHA * x
    for c in range(N_FF_CHUNKS):
        cols = slice(FF_CHUNK * c, FF_CHUNK * (c + 1))
        gate = jnp.dot(xb, wup_ref[:, cols], preferred_element_type=F32)
        up = jnp.dot(xb, wup_ref[:, D_FF + FF_CHUNK * c:D_FF + FF_CHUNK * (c + 1)], preferred_element_type=F32)
        act = gate * _sigmoid(gate) * up
        acc = acc + jnp.dot(act.astype(BF16), wdn_ref[cols, :], preferred_element_type=F32)
    o_ref[...] = _layer_norm(acc, g_ref[...], b_ref[...])


def _ffn_ln(x, wup_all, wdn_all, layer, gain, bias, tm):
    n = x.shape[0]
    layer_spec = lambda *shape: pl.BlockSpec((None,) + shape, lambda i: (layer, 0, 0), pipeline_mode=pl.Buffered(1))
    return pl.pallas_call(
        _ffn_body,
        grid=(n // tm,),
        in_specs=[pl.BlockSpec((tm, D_MODEL), lambda i: (i, 0)),
                  layer_spec(D_MODEL, 2 * D_FF), layer_spec(D_FF, D_MODEL),
                  _const_spec((1, D_MODEL)), _const_spec((1, D_MODEL))],
        out_specs=pl.BlockSpec((tm, D_MODEL), lambda i: (i, 0)),
        out_shape=jax.ShapeDtypeStruct((n, D_MODEL), F32),
        compiler_params=_params(1),
        name="ffn_ln",
    )(x, wup_all, wdn_all, gain, bias)


def _qkv_body(x_ref, w_ref, b_ref, cos_ref, sin_ref, q_ref, k_ref, v_ref):
    half = HEAD_DIM // 2
    chunk = 2 * LANES
    n_chunks = x_ref.shape[0] // chunk
    nt = (((1,), (1,)), ((), ()))

    def project(h):
        xb = x_ref[h * chunk:(h + 1) * chunk, :].astype(BF16)
        return (lax.dot_general(w_ref[:Q_DIM, :], xb, nt, preferred_element_type=F32),
                lax.dot_general(w_ref[Q_DIM:, :], xb, nt, preferred_element_type=F32))

    def rope(t, cos, sin):
        rot = t.reshape(t.shape[0] // HEAD_DIM, 2, half, chunk)
        first, second = rot[:, 0], rot[:, 1]
        return jnp.stack([first * cos - second * sin, second * cos + first * sin], axis=1).reshape(t.shape)

    nxt = project(0)
    for h in range(n_chunks):
        cols = slice(h * chunk, (h + 1) * chunk)
        q_t, kv_t = nxt[0] + b_ref[:Q_DIM, :], nxt[1] + b_ref[Q_DIM:, :]
        if h + 1 < n_chunks:
            nxt = project(h + 1)
        cos, sin = cos_ref[:, cols], sin_ref[:, cols]
        q_ref[:, cols] = (rope(q_t, cos, sin) * (HEAD_DIM ** -0.5 * LOG2E)).astype(BF16)
        k_ref[:, cols] = rope(kv_t[:KV_DIM], cos, sin)
        v_ref[:, cols] = kv_t[KV_DIM:]


def _qkv_rope(x, w_t, b_col, tabs, tm, tab_tiles):
    n = x.shape[0]
    tab_spec = pl.BlockSpec((HEAD_DIM // 2, tm), lambda i: (0, i % tab_tiles))
    col = lambda rows: pl.BlockSpec((rows, tm), lambda i: (0, i))
    return pl.pallas_call(
        _qkv_body,
        grid=(n // tm,),
        in_specs=[pl.BlockSpec((tm, D_MODEL), lambda i: (i, 0)),
                  _const_spec((QKV_DIM, D_MODEL)), _const_spec((QKV_DIM, 1)), tab_spec, tab_spec],
        out_specs=[col(Q_DIM), col(KV_DIM), col(KV_DIM)],
        out_shape=[jax.ShapeDtypeStruct((Q_DIM, n), BF16),
                   jax.ShapeDtypeStruct((KV_DIM, n), F32),
                   jax.ShapeDtypeStruct((KV_DIM, n), F32)],
        compiler_params=_params(1),
        name="qkv_rope",
    )(x, w_t, b_col, *tabs)


def _rope_tables(pos):
    half = HEAD_DIM // 2
    inv = ROPE_THETA ** (-np.arange(half, dtype=np.float64) / half)
    ang = inv[:, None] * np.asarray(pos, np.float64)[None, :]
    return tuple(jnp.asarray(t.astype(np.float32)) for t in (np.cos(ang), np.sin(ang)))


PAIR = 2 * CHUNK


def _attn_body(x_ref, q_ref, kp_ref, kc_ref, vp_ref, vc_ref, sink_ref, wot_ref, bo_ref, g_ref, b_ref,
               out_ref, ot_scr, *, tq, has_start):
    span = WINDOW + PAIR
    k_t = jnp.concatenate([kp_ref[...], kc_ref[...]], axis=1)
    v_t = jnp.concatenate([vp_ref[...], vc_ref[...]], axis=1).astype(BF16)
    k_rows = [k_t[m * HEAD_DIM:(m + 1) * HEAD_DIM].T.astype(BF16) for m in range(N_KV_HEADS)]
    ones_rows = jnp.ones((BF16_SUBLANES, v_t.shape[1]), BF16)
    v_aug = [jnp.concatenate([v_t[m * HEAD_DIM:(m + 1) * HEAD_DIM], ones_rows], axis=0) for m in range(N_KV_HEADS)]
    n_cols = GQA_GROUP * PAIR
    tok = lax.broadcasted_iota(jnp.int32, (CHUNK, n_cols), 1) & (PAIR - 1)
    first_chunk = tok < CHUNK
    if has_start:
        after_start = jnp.broadcast_to(pl.program_id(1) > 0, first_chunk.shape)

    def scores(pr, m):
        heads = range(GQA_GROUP * m, GQA_GROUP * (m + 1))
        q_m = jnp.concatenate([q_ref[h * HEAD_DIM:(h + 1) * HEAD_DIM, pr * PAIR:(pr + 1) * PAIR] for h in heads], axis=1)
        return jnp.dot(k_rows[m][pr * PAIR:pr * PAIR + span], q_m, preferred_element_type=F32)

    steps = [(pr, m) for pr in range(tq // PAIR) for m in range(N_KV_HEADS)]
    proj_pairs = min(2, tq // PAIR)
    s_next = scores(*steps[0])
    for idx, (pr, m) in enumerate(steps):
        s = s_next
        if idx + 1 < len(steps):
            s_next = scores(*steps[idx + 1])
        rows = slice(pr * PAIR, (pr + 1) * PAIR)
        oldest, older = s[:CHUNK], s[CHUNK:2 * CHUNK]
        if has_start and pr == 0:
            oldest = jnp.where(after_start, oldest, NEG_INF)
            older = jnp.where(after_start, older, NEG_INF)
        edge = jnp.where(first_chunk, oldest, s[3 * CHUNK:])
        s = jnp.concatenate([edge, older, s[2 * CHUNK:3 * CHUNK]], axis=0)
        sink = sink_ref[m] * LOG2E
        mx = jnp.maximum(jnp.max(s, axis=0, keepdims=True), sink)
        e = jnp.exp2(s - mx)
        e_edge = e[:CHUNK]
        p_t = jnp.concatenate([jnp.where(first_chunk, e_edge, 0.0), e[CHUNK:],
                               jnp.where(first_chunk, 0.0, e_edge)], axis=0).astype(BF16)
        o_aug = jnp.dot(v_aug[m][:, pr * PAIR:pr * PAIR + span], p_t, preferred_element_type=F32)
        den = o_aug[HEAD_DIM:HEAD_DIM + 1] + jnp.exp2(sink - mx)
        o_t = o_aug[:HEAD_DIM] * (1.0 / den)
        for hb in range(GQA_GROUP):
            head = GQA_GROUP * m + hb
            ot_scr[head * HEAD_DIM:(head + 1) * HEAD_DIM, rows] = o_t[:, hb * PAIR:(hb + 1) * PAIR].astype(BF16)
        if m == N_KV_HEADS - 1 and (pr + 1) % proj_pairs == 0:
            blk = slice((pr + 1 - proj_pairs) * PAIR, (pr + 1) * PAIR)
            proj_t = jnp.dot(wot_ref[...], ot_scr[:, blk], preferred_element_type=F32)
            y = DN_ALPHA * x_ref[blk, :] + proj_t.T + bo_ref[...]
            out_ref[blk, :] = _layer_norm(y, g_ref[...], b_ref[...])


def _attn_proj_ln(x, q_t, k_prev, k_cur, v_prev, v_cur, sink_rows, wo_t, bo, gain, bias, *, n_batch, tq, prev_map, has_start):
    n = x.shape[0]
    tiles = n // (n_batch * tq)
    cur = lambda b, i: (b * tiles + i, 0)
    cur_t = lambda b, i: (0, b * tiles + i)
    return pl.pallas_call(
        functools.partial(_attn_body, tq=tq, has_start=has_start),
        grid=(n_batch, tiles),
        in_specs=[pl.BlockSpec((tq, D_MODEL), cur), pl.BlockSpec((Q_DIM, tq), cur_t),
                  pl.BlockSpec((KV_DIM, WINDOW), prev_map), pl.BlockSpec((KV_DIM, tq), cur_t),
                  pl.BlockSpec((KV_DIM, WINDOW), prev_map), pl.BlockSpec((KV_DIM, tq), cur_t),
                  _const_spec((N_KV_HEADS, 1, GQA_GROUP * PAIR)),
                  _const_spec((D_MODEL, Q_DIM)), _const_spec((1, D_MODEL)),
                  _const_spec((1, D_MODEL)), _const_spec((1, D_MODEL))],
        out_specs=pl.BlockSpec((tq, D_MODEL), cur),
        out_shape=jax.ShapeDtypeStruct((n, D_MODEL), F32),
        scratch_shapes=[pltpu.VMEM((Q_DIM, tq), BF16)],
        compiler_params=_params(2),
        name="attn_proj_ln",
    )(x, q_t, k_prev, k_cur, v_prev, v_cur, sink_rows, wo_t, bo, gain, bias)


def _ssm_prep_group(k, pcol_ref, prow_ref, bcol_ref, brow_ref, ccol_ref, dcol_ref, t_ref, wt_ref, v_ref, are_ref, aim_ref):
    def cmul(x_re, x_im, y_re, y_im):
        return x_re * y_re - x_im * y_im, x_re * y_im + x_im * y_re

    def a_bar(a_re, a_im, dt):
        mag = jnp.exp(dt * a_re)
        ang = dt * a_im
        return mag * jnp.cos(ang), mag * jnp.sin(ang)

    def zoh_factor(l_re, l_im, a_re, a_im):
        den = a_re * a_re + a_im * a_im
        n_re = l_re - 1.0
        return (n_re * a_re + l_im * a_im) / den, (l_im * a_re - n_re * a_im) / den

    n_bits = SSM_BLOCK.bit_length() - 1
    pc = pcol_ref[k]
    ar_c, ai_c, dt_c = pc[:, 0:1], pc[:, 1:2], jnp.exp(pc[:, 2:3])
    lam_c = a_bar(ar_c, ai_c, dt_c)
    fr_c, fi_c = zoh_factor(*lam_c, ar_c, ai_c)
    bc = bcol_ref[k]
    bbr_c = fr_c * bc[:, :SSM_GROUP] - fi_c * bc[:, SSM_GROUP:]
    bbi_c = fr_c * bc[:, SSM_GROUP:] + fi_c * bc[:, :SSM_GROUP]
    pr = prow_ref[k]
    ar_r, ai_r, dt_r = pr[0:1], pr[1:2], jnp.exp(pr[2:3])
    lam_r = a_bar(ar_r, ai_r, dt_r)
    fr_r, fi_r = zoh_factor(*lam_r, ar_r, ai_r)
    br = brow_ref[k]
    bbr_r = fr_r * br[:SSM_GROUP] - fi_r * br[SSM_GROUP:]
    bbi_r = fr_r * br[SSM_GROUP:] + fi_r * br[:SSM_GROUP]
    cc = ccol_ref[k]

    squares = [lam_c]
    for _ in range(n_bits - 1):
        squares.append(cmul(*squares[-1], *squares[-1]))

    def lane_pow(exps):
        acc = None
        for k, (q_re, q_im) in enumerate(squares):
            bit = ((exps >> k) & 1) == 1
            f_re, f_im = jnp.where(bit, q_re, 1.0), jnp.where(bit, q_im, 0.0)
            acc = (f_re, f_im) if acc is None else cmul(*acc, f_re, f_im)
        return acc

    lane = lax.broadcasted_iota(jnp.int32, (SSM_STATE, LANES), 1)
    lo = lane < SSM_BLOCK
    tau = lane & (SSM_BLOCK - 1)
    l0r, l0i = lane_pow(tau)
    l1r, l1i = cmul(l0r, l0i, *lam_c)
    lrr, lri = lane_pow(SSM_BLOCK - 1 - tau)

    g_re, g_im = [], []
    for j in range(SSM_GROUP // 2):
        sl = slice(j * LANES, (j + 1) * LANES)
        cr = jnp.where(lo, cc[:, 2 * j:2 * j + 1], cc[:, 2 * j + 1:2 * j + 2])
        ci = jnp.where(lo, cc[:, SSM_GROUP + 2 * j:SSM_GROUP + 2 * j + 1],
                       cc[:, SSM_GROUP + 2 * j + 1:SSM_GROUP + 2 * j + 2])
        g_re.append(cr * l0r - ci * l0i)
        g_im.append(-(cr * l0i + ci * l0r))
        v_ref[k, :SSM_STATE, sl] = (cr * l1r - ci * l1i).astype(BF16)
        v_ref[k, SSM_STATE:, sl] = (-(cr * l1i + ci * l1r)).astype(BF16)
        b_r = jnp.where(lo, bbr_c[:, 2 * j:2 * j + 1], bbr_c[:, 2 * j + 1:2 * j + 2])
        b_i = jnp.where(lo, bbi_c[:, 2 * j:2 * j + 1], bbi_c[:, 2 * j + 1:2 * j + 2])
        wt_ref[k, :SSM_STATE, sl] = (b_r * lrr - b_i * lri).astype(BF16)
        wt_ref[k, SSM_STATE:, sl] = (b_r * lri + b_i * lrr).astype(BF16)
    g_re = jnp.concatenate(g_re, axis=1)
    g_im = jnp.concatenate(g_im, axis=1)
    kvec = (jnp.dot(bbr_r, g_re, preferred_element_type=F32, precision=lax.Precision.HIGHEST)
            + jnp.dot(bbi_r, g_im, preferred_element_type=F32, precision=lax.Precision.HIGHEST))
    klane = lax.broadcasted_iota(jnp.int32, kvec.shape, 1)
    krow = lax.broadcasted_iota(jnp.int32, kvec.shape, 0)
    kvec = kvec + jnp.where(klane == krow * SSM_BLOCK, dcol_ref[k], 0.0)

    t_in = lax.broadcasted_iota(jnp.int32, (SSM_BLOCK, LANES), 0)
    causal = (lax.broadcasted_iota(jnp.int32, (SSM_BLOCK, LANES), 1) & (SSM_BLOCK - 1)) >= t_in
    for c in range(SSM_GROUP):
        for j in range(SSM_GROUP // 2):
            src = jnp.broadcast_to(kvec[c:c + 1, j * LANES:(j + 1) * LANES], (SSM_BLOCK, LANES))
            shifted = pltpu.roll(src, 0, 1, stride=1, stride_axis=0)
            t_ref[k, c * SSM_BLOCK:(c + 1) * SSM_BLOCK, j * LANES:(j + 1) * LANES] = (
                jnp.where(causal, shifted, 0.0).astype(BF16))
    a_re, a_im = lam_r
    for _ in range(n_bits):
        a_re, a_im = cmul(a_re, a_im, a_re, a_im)
    are_ref[k] = a_re
    aim_ref[k] = a_im


def _ssm_prep_inputs(log_dt, a_re, a_im, b_re, b_im, c_re, c_im, d):
    g, p, cg = SSM_GROUPS, SSM_STATE, SSM_GROUP
    params = jnp.stack([a_re, a_im, jnp.broadcast_to(log_dt[:, None], (g, p))], axis=-1)
    bcol = jnp.concatenate([b_re, b_im], axis=-1)
    ccol = jnp.concatenate([c_re.transpose(0, 2, 1), c_im.transpose(0, 2, 1)], axis=-1)
    return params, params.transpose(0, 2, 1), bcol, bcol.transpose(0, 2, 1), ccol, d.reshape(g, cg, 1)


def _ssm_in_body(x_ref, w_ref, b_ref, *rest, tm, groups):
    prep_refs, o_ref = rest[:-1 - 5 * bool(groups)], rest[-1 - 5 * bool(groups)]
    xb = x_ref[...].reshape(tm, D_MODEL).astype(BF16)
    ut = lax.dot_general(w_ref[...], xb, (((1,), (1,)), ((), ())), preferred_element_type=F32)
    ut = ut + b_ref[...]
    if len(o_ref.shape) == 2:
        o_ref[...] = ut
    else:
        for j in range(tm // LANES):
            o_ref[:, SUBLANES * j:SUBLANES * (j + 1), :] = (
                ut[:, j * LANES:(j + 1) * LANES].reshape(D_MODEL // SUBLANES, SUBLANES, LANES))
    for k in range(groups):
        _ssm_prep_group(k, *prep_refs, *rest[-5:])


def _ssm_in(x, x_spec, n_tiles, w_t, b_col, tm, prep_inputs=()):
    g, p, cg, n = SSM_GROUPS, SSM_STATE, SSM_GROUP, SSM_BLOCK_DIM
    groups = g // n_tiles if prep_inputs else 0
    spec = lambda *shape: pl.BlockSpec((groups,) + shape, lambda i: (i, 0, 0))
    prep_in = [spec(p, 3), spec(3, p), spec(p, 2 * cg), spec(2 * cg, p), spec(p, 2 * cg), spec(cg, 1)]
    prep_out = [spec(n, n), spec(2 * p, n), spec(2 * p, n), spec(1, p), spec(1, p)]
    prep_shapes = ([jax.ShapeDtypeStruct((g, n, n), BF16)] + [jax.ShapeDtypeStruct((g, 2 * p, n), BF16)] * 2
                   + [jax.ShapeDtypeStruct((g, 1, p), F32)] * 2)
    if groups:
        rows = SUBLANES * tm // LANES
        u_spec = pl.BlockSpec((D_MODEL // SUBLANES, rows, LANES), lambda i: (0, i, 0))
        u_shape = jax.ShapeDtypeStruct((D_MODEL // SUBLANES, n_tiles * rows, LANES), F32)
    else:
        u_spec = pl.BlockSpec((D_MODEL, tm), lambda i: (0, i))
        u_shape = jax.ShapeDtypeStruct((D_MODEL, n_tiles * tm), F32)
    return pl.pallas_call(
        functools.partial(_ssm_in_body, tm=tm, groups=groups),
        grid=(n_tiles,),
        in_specs=[x_spec, _const_spec((D_MODEL, D_MODEL)), _const_spec((D_MODEL, 1))] + (prep_in if groups else []),
        out_specs=[u_spec] + (prep_out if groups else []),
        out_shape=[u_shape] + (prep_shapes if groups else []),
        compiler_params=_params(1),
        name="ssm_in",
    )(x, w_t, b_col, *prep_inputs)


def _plane_rows(k, c, half_rows):
    octet = (SSM_GROUP // SUBLANES) * k + c // SUBLANES
    return pl.ds(octet * SUBLANES * half_rows + c % SUBLANES, half_rows, stride=SUBLANES)


def _ssm_core_group(k, x_ref, xs_ref, s0re_ref, s0im_ref, t_ref, wt_ref, v_ref, are_ref, aim_ref,
                    z_ref, zs_ref, sre_ref, sim_ref, pre_ref, pim_ref, s_in, *, half_rows, n_seq):
    n_prompt = 2 * half_rows
    n_rows = s_in.shape[1]
    ch0 = SSM_GROUP * k
    lo = lax.broadcasted_iota(jnp.int32, (half_rows, LANES), 1) < SSM_BLOCK
    lo_s = lax.broadcasted_iota(jnp.int32, (n_seq, LANES), 1) < SSM_BLOCK
    pad = jnp.zeros((n_rows - n_prompt - n_seq, LANES), F32)
    cols = []
    for j in range(SSM_GROUP // 2):
        xa = x_ref[_plane_rows(k, 2 * j, half_rows), :]
        xb = x_ref[_plane_rows(k, 2 * j + 1, half_rows), :]
        first = jnp.where(lo, xa, pltpu.roll(xb, SSM_BLOCK, 1))
        second = jnp.where(lo, pltpu.roll(xa, SSM_BLOCK, 1), xb)
        samp = jnp.where(lo_s, xs_ref[ch0 + 2 * j], pltpu.roll(xs_ref[ch0 + 2 * j + 1], SSM_BLOCK, 1))
        cols.append(jnp.concatenate([first, second, samp, pad], axis=0).astype(BF16))
    u = jnp.concatenate(cols, axis=1)

    inj = lax.dot_general(u, wt_ref[k], (((1,), (1,)), ((), ())), preferred_element_type=F32)
    inj_re, inj_im = inj[:, :SSM_STATE], inj[:, SSM_STATE:]
    y_u = jnp.dot(u, t_ref[k], preferred_element_type=F32)
    a_re, a_im = are_ref[k], aim_ref[k]

    s_re = jnp.zeros((n_seq, SSM_STATE), F32)
    s_im = jnp.zeros((n_seq, SSM_STATE), F32)
    for n in range(n_prompt // n_seq):
        r = slice(n * n_seq, (n + 1) * n_seq)
        s_in[k, r, :SSM_STATE] = s_re
        s_in[k, r, SSM_STATE:] = s_im
        s_re, s_im = (a_re * s_re - a_im * s_im + inj_re[r], a_re * s_im + a_im * s_re + inj_im[r])
    pre_ref[k] = s_re
    pim_ref[k] = s_im
    rs = slice(n_prompt, n_prompt + n_seq)
    s0_re, s0_im = s0re_ref[k], s0im_ref[k]
    s_in[k, rs, :SSM_STATE] = s0_re
    s_in[k, rs, SSM_STATE:] = s0_im
    s_in[k, n_prompt + n_seq:, :] = jnp.zeros((n_rows - n_prompt - n_seq, 2 * SSM_STATE), F32)
    sre_ref[k] = a_re * s0_re - a_im * s0_im + inj_re[rs]
    sim_ref[k] = a_re * s0_im + a_im * s0_re + inj_im[rs]

    s_bf = s_in[k].astype(BF16)
    width = 2 * LANES
    for h in range(SSM_BLOCK_DIM // width):
        y = y_u[:, h * width:(h + 1) * width] + jnp.dot(s_bf, v_ref[k, :, h * width:(h + 1) * width],
                                                        preferred_element_type=F32)
        z = _gelu_tanh(y)
        for jj in range(width // LANES):
            j = h * (width // LANES) + jj
            blk = z[:, jj * LANES:(jj + 1) * LANES]
            first, second = blk[:half_rows], blk[half_rows:n_prompt]
            z_ref[_plane_rows(k, 2 * j, half_rows), :] = jnp.where(lo, first, pltpu.roll(second, SSM_BLOCK, 1))
            z_ref[_plane_rows(k, 2 * j + 1, half_rows), :] = jnp.where(lo, pltpu.roll(first, SSM_BLOCK, 1), second)
            samp = blk[rs]
            zs_ref[ch0 + 2 * j] = samp
            zs_ref[ch0 + 2 * j + 1] = pltpu.roll(samp, SSM_BLOCK, 1)


def _ssm_core_body(*refs, half_rows, n_seq, groups):
    for k in range(groups):
        _ssm_core_group(k, *refs, half_rows=half_rows, n_seq=n_seq)


def _ssm_core(ut, uts3, s0_re, s0_im, ops, *, half_rows, n_seq):
    t_op, wt, v_op, a_re, a_im = ops
    g, p, cg, n = SSM_GROUPS, SSM_STATE, SSM_GROUP, SSM_BLOCK_DIM
    n_rows = 2 * half_rows + 2 * n_seq
    groups = SSM_GROUPS_PER_STEP
    grp = lambda *shape: pl.BlockSpec((groups,) + shape, lambda i: (i, 0, 0))
    chan = lambda r: pl.BlockSpec((groups * cg, r, LANES), lambda i: (i, 0, 0))
    rows = pl.BlockSpec((groups * cg * half_rows, LANES), lambda i: (i, 0))
    state = jax.ShapeDtypeStruct((g, n_seq, p), F32)
    return pl.pallas_call(
        functools.partial(_ssm_core_body, half_rows=half_rows, n_seq=n_seq, groups=groups),
        grid=(g // groups,),
        in_specs=[rows, chan(n_seq), grp(n_seq, p), grp(n_seq, p),
                  grp(n, n), grp(2 * p, n), grp(2 * p, n), grp(1, p), grp(1, p)],
        out_specs=[rows, chan(n_seq), grp(n_seq, p), grp(n_seq, p), grp(n_seq, p), grp(n_seq, p)],
        out_shape=[jax.ShapeDtypeStruct(ut.shape, F32), jax.ShapeDtypeStruct(uts3.shape, F32),
                   state, state, state, state],
        scratch_shapes=[pltpu.VMEM((groups, n_rows, 2 * p), F32)],
        compiler_params=_params(1),
        name="ssm_core",
    )(ut, uts3, s0_re, s0_im, t_op, wt, v_op, a_re, a_im)


def _glu_body(x_ref, zt_ref, w_ref, b_ref, g_ref, bb_ref, o_ref, *, tm):
    chunk = 2 * LANES
    lead = chunk // (tm // x_ref.shape[0])

    def gate_values(h):
        if len(zt_ref.shape) == 2:
            zt = zt_ref[:, h * chunk:(h + 1) * chunk]
        else:
            zt = jnp.concatenate(
                [zt_ref[:, SUBLANES * j:SUBLANES * (j + 1), :].reshape(D_MODEL, LANES)
                 for j in range(h * chunk // LANES, (h + 1) * chunk // LANES)], axis=1).astype(BF16)
        return (jnp.dot(w_ref[:D_MODEL, :], zt, preferred_element_type=F32),
                jnp.dot(w_ref[D_MODEL:, :], zt, preferred_element_type=F32))

    gv_next = gate_values(0)
    for h in range(tm // chunk):
        value, gate = gv_next[0] + b_ref[:D_MODEL, :], gv_next[1] + b_ref[D_MODEL:, :]
        if h + 1 < tm // chunk:
            gv_next = gate_values(h + 1)
        mixed = value * _sigmoid(gate)
        x = x_ref[h * lead:(h + 1) * lead].reshape(chunk, D_MODEL)
        y = DN_ALPHA * x + mixed.T
        o_ref[h * lead:(h + 1) * lead] = _layer_norm(y, g_ref[...], bb_ref[...]).reshape((lead,) + o_ref.shape[1:])


def _glu_ln(x, x_spec, n_tiles, zt, w_t, b_col, gain, bias, tm):
    if zt.ndim == 3:
        z_spec = pl.BlockSpec((D_MODEL // SUBLANES, SUBLANES * tm // LANES, LANES), lambda i: (0, i, 0))
    else:
        z_spec = pl.BlockSpec((D_MODEL, tm), lambda i: (0, i))
    return pl.pallas_call(
        functools.partial(_glu_body, tm=tm),
        grid=(n_tiles,),
        in_specs=[x_spec, z_spec,
                  _const_spec((2 * D_MODEL, D_MODEL)), _const_spec((2 * D_MODEL, 1)),
                  _const_spec((1, D_MODEL)), _const_spec((1, D_MODEL))],
        out_specs=x_spec,
        out_shape=jax.ShapeDtypeStruct(x.shape, F32),
        compiler_params=_params(1),
        name="glu_ln",
    )(x, zt, w_t, b_col, gain, bias)


def kernel(x_prompt, x_sample, cache_k, cache_v, state_ssm_re, state_ssm_im, attn_w_qkv, attn_b_qkv, attn_sinks, attn_w_o, attn_b_o, ssm_w_in, ssm_b_in, ssm_log_dt, ssm_a_re, ssm_a_im, ssm_b_re, ssm_b_im, ssm_c_re, ssm_c_im, ssm_d, ssm_w_glu, ssm_b_glu, ffn_w_up, ffn_w_down, ln_gain, ln_bias):
    n_b, seq, _ = x_prompt.shape
    n_s, seq_s, _ = x_sample.shape
    n_p, n_d = n_b * seq, n_s * seq_s
    blocks = seq // SSM_BLOCK
    half_rows = (blocks // 2) * n_b
    ssm_tm = 2 * SSM_BLOCK * n_b

    xp = x_prompt.reshape(n_p, D_MODEL)
    xs = x_sample.reshape(n_d, D_MODEL)
    tabs_p = _rope_tables(np.arange(seq))
    tabs_s = _rope_tables(np.tile(PAST_LEN + np.arange(seq_s), n_s))

    ssm_view = (n_b, 2, blocks // 2, SSM_BLOCK, D_MODEL)
    ssm_spec = pl.BlockSpec((n_b, 2, 1, SSM_BLOCK, D_MODEL), lambda i: (0, 0, i, 0, 0))
    flat_spec = pl.BlockSpec((n_d, D_MODEL), lambda i: (i, 0))

    w_up_all = ffn_w_up.astype(BF16)
    w_dn_all = ffn_w_down.astype(BF16)
    new_k_p, new_v_p, new_re_p, new_im_p = [], [], [], []
    new_k_s, new_v_s, new_re_s, new_im_s = [], [], [], []
    for i in range(DEPTH):
        l = i // 2
        gain = ln_gain[i][:, None, :]
        bias = ln_bias[i][:, None, :]
        if i % 2 == 0:
            w_qkv_t = attn_w_qkv[l].T.astype(BF16)
            b_qkv = attn_b_qkv[l][:, None]
            w_o_t = attn_w_o[l].T.astype(BF16)
            b_o = attn_b_o[l][None]
            sink_rows = jnp.repeat(attn_sinks[l].astype(F32), PAIR).reshape(N_KV_HEADS, 1, GQA_GROUP * PAIR)
            qp, kp, vp = _qkv_rope(xp, w_qkv_t, b_qkv, tabs_p, QKV_TILE, seq // QKV_TILE)
            qs, ks, vs = _qkv_rope(xs, w_qkv_t, b_qkv, tabs_s, n_d, 1)
            prev_p = lambda b, t: (0, jnp.maximum(b * (seq // WINDOW) + t * (ATTN_TILE // WINDOW) - 1, 0))
            xp = _attn_proj_ln(xp, qp, kp, kp, vp, vp, sink_rows, w_o_t, b_o, gain[0], bias[0],
                               n_batch=n_b, tq=ATTN_TILE, prev_map=prev_p, has_start=True)
            pad_rows = lambda a: jnp.pad(a.reshape(n_s, seq_s, -1), ((0, 0), (0, PAIR - seq_s), (0, 0))).reshape(n_s * PAIR, -1)
            pad_cols = lambda a: jnp.pad(a.reshape(-1, n_s, seq_s), ((0, 0), (0, 0), (0, PAIR - seq_s))).reshape(-1, n_s * PAIR)
            ck_t = cache_k[l].reshape(n_s * WINDOW, KV_DIM).T
            cv_t = cache_v[l].reshape(n_s * WINDOW, KV_DIM).T
            xs = _attn_proj_ln(pad_rows(xs), pad_cols(qs), ck_t, pad_cols(ks), cv_t, pad_cols(vs), sink_rows, w_o_t, b_o,
                               gain[0], bias[0], n_batch=n_s, tq=PAIR, prev_map=lambda b, t: (0, b), has_start=False)
            xs = xs.reshape(n_s, PAIR, D_MODEL)[:, :seq_s].reshape(n_d, D_MODEL)
            last = lambda a: (jnp.stack([a[:, (b + 1) * seq - WINDOW:(b + 1) * seq] for b in range(n_b)])
                              .transpose(0, 2, 1).reshape(n_b, WINDOW, N_KV_HEADS, HEAD_DIM))
            new_k_p.append(last(kp))
            new_v_p.append(last(vp))
            ks4 = ks.T.reshape(n_s, seq_s, N_KV_HEADS, HEAD_DIM)
            vs4 = vs.T.reshape(n_s, seq_s, N_KV_HEADS, HEAD_DIM)
            new_k_s.append(jnp.concatenate([cache_k[l], ks4], axis=1)[:, -WINDOW:])
            new_v_s.append(jnp.concatenate([cache_v[l], vs4], axis=1)[:, -WINDOW:])
        else:
            w_in_t = ssm_w_in[l].T.astype(BF16)
            b_in = ssm_b_in[l][:, None]
            w_glu_t = ssm_w_glu[l].T.astype(BF16)
            b_glu = ssm_b_glu[l][:, None]
            prep_inputs = _ssm_prep_inputs(ssm_log_dt[l], ssm_a_re[l], ssm_a_im[l], ssm_b_re[l], ssm_b_im[l],
                                           ssm_c_re[l], ssm_c_im[l], ssm_d[l])
            ut, *ops = _ssm_in(xp.reshape(ssm_view), ssm_spec, blocks // 2, w_in_t, b_in, ssm_tm, prep_inputs)
            uts, = _ssm_in(xs, flat_spec, 1, w_in_t, b_in, n_d)
            uts3 = jnp.pad(uts.reshape(D_MODEL, n_s, seq_s), ((0, 0), (0, 0), (0, LANES - seq_s)))
            s0_re = state_ssm_re[l].transpose(1, 0, 2)
            s0_im = state_ssm_im[l].transpose(1, 0, 2)
            zt, zs3, s_re, s_im, p_re, p_im = _ssm_core(ut.reshape(-1, LANES), uts3, s0_re, s0_im, ops,
                                                        half_rows=half_rows, n_seq=n_s)
            zst = zs3[:, :, :seq_s].reshape(D_MODEL, n_d).astype(BF16)
            xp = _glu_ln(xp.reshape(ssm_view), ssm_spec, blocks // 2, zt.reshape(ut.shape), w_glu_t, b_glu,
                         gain[0], bias[0], ssm_tm).reshape(n_p, D_MODEL)
            xs = _glu_ln(xs, flat_spec, 1, zst, w_glu_t, b_glu, gain[0], bias[0], n_d)
            new_re_p.append(p_re.transpose(1, 0, 2))
            new_im_p.append(p_im.transpose(1, 0, 2))
            new_re_s.append(s_re.transpose(1, 0, 2))
            new_im_s.append(s_im.transpose(1, 0, 2))
        xp = _ffn_ln(xp, w_up_all, w_dn_all, i, gain[1], bias[1], FFN_TILE)
        xs = _ffn_ln(xs, w_up_all, w_dn_all, i, gain[1], bias[1], n_d)
    return (xp.reshape(x_prompt.shape), xs.reshape(x_sample.shape),
            jnp.stack(new_k_p), jnp.stack(new_v_p), jnp.stack(new_re_p), jnp.stack(new_im_p),
            jnp.stack(new_k_s), jnp.stack(new_v_s), jnp.stack(new_re_s), jnp.stack(new_im_s))
```

```python
import functools

import jax
import jax.numpy as jnp
import numpy as np
from jax import lax
from jax.experimental import pallas as pl
from jax.experimental.pallas import tpu as pltpu

F32 = jnp.float32
BF16 = jnp.bfloat16

D_MODEL = 1024
DEPTH = 4
CHUNK = 64
N_HEADS = 16
N_KV_HEADS = 2
HEAD_DIM = 64
GQA_GROUP = N_HEADS // N_KV_HEADS
Q_DIM = N_HEADS * HEAD_DIM
KV_DIM = N_KV_HEADS * HEAD_DIM
QKV_DIM = Q_DIM + 2 * KV_DIM
WINDOW = 128
PAST_LEN = 4096
ROPE_THETA = 10000.0
SSM_GROUP = 16
SSM_GROUPS = D_MODEL // SSM_GROUP
SSM_STATE = 64
D_FF = 2816
DN_ALPHA = (2.0 * DEPTH) ** 0.25
LN_EPS = 1e-5
NEG_INF = -1e30
LOG2E = 1.4426950408889634

LANES = 128
SUBLANES = 8
BF16_SUBLANES = 16
FF_CHUNK = 256
N_FF_CHUNKS = D_FF // FF_CHUNK
SSM_BLOCK = CHUNK
SSM_BLOCK_DIM = SSM_GROUP * SSM_BLOCK
SSM_GROUPS_PER_STEP = 2
LOOKAHEAD = 2
FFN_TILE = 1024
QKV_TILE = 2048
ATTN_TILE = 2048
VMEM_LIMIT = 56 * 2 ** 20


def _params(n_axes):
    return pltpu.CompilerParams(dimension_semantics=("arbitrary",) * n_axes, vmem_limit_bytes=VMEM_LIMIT)


def _const_spec(shape):
    zeros = (0,) * len(shape)
    return pl.BlockSpec(shape, lambda *_: zeros, pipeline_mode=pl.Buffered(1))


def _sigmoid(x):
    return 0.5 * jnp.tanh(0.5 * x) + 0.5


def _gelu_tanh(x):
    k = (2.0 / np.pi) ** 0.5
    half_x = 0.5 * x
    return half_x * jnp.tanh(x * (k + (k * 0.044715) * (x * x))) + half_x


def _layer_norm(y, gain, bias):
    mu = jnp.mean(y, axis=-1, keepdims=True)
    d = y - mu
    var = jnp.mean(d * d, axis=-1, keepdims=True)
    return d * lax.rsqrt(var + LN_EPS) * gain + bias


def _ffn_body(x_ref, wup_ref, wdn_ref, g_ref, b_ref, o_ref):
    x = x_ref[...]
    xb = x.astype(BF16)
    acc = DN_ALPHA * x
    for c in range(N_FF_CHUNKS):
        cols = slice(FF_CHUNK * c, FF_CHUNK * (c + 1))
        gate = jnp.dot(xb, wup_ref[:, cols], preferred_element_type=F32)
        up = jnp.dot(xb, wup_ref[:, D_FF + FF_CHUNK * c:D_FF + FF_CHUNK * (c + 1)], preferred_element_type=F32)
        act = gate * _sigmoid(gate) * up
        acc = acc + jnp.dot(act.astype(BF16), wdn_ref[cols, :], preferred_element_type=F32)
    o_ref[...] = _layer_norm(acc, g_ref[...], b_ref[...])


def _ffn_ln(x, wup_all, wdn_all, layer, gain, bias, tm):
    n = x.shape[0]
    layer_spec = lambda *shape: pl.BlockSpec((None,) + shape, lambda i: (layer, 0, 0), pipeline_mode=pl.Buffered(1))
    return pl.pallas_call(
        _ffn_body,
        grid=(n // tm,),
        in_specs=[pl.BlockSpec((tm, D_MODEL), lambda i: (i, 0)),
                  layer_spec(D_MODEL, 2 * D_FF), layer_spec(D_FF, D_MODEL),
                  _const_spec((1, D_MODEL)), _const_spec((1, D_MODEL))],
        out_specs=pl.BlockSpec((tm, D_MODEL), lambda i: (i, 0)),
        out_shape=jax.ShapeDtypeStruct((n, D_MODEL), F32),
        compiler_params=_params(1),
        name="ffn_ln",
    )(x, wup_all, wdn_all, gain, bias)


def _qkv_body(x_ref, w_ref, b_ref, cos_ref, sin_ref, q_ref, k_ref, v_ref):
    half = HEAD_DIM // 2
    chunk = 2 * LANES
    n_chunks = x_ref.shape[0] // chunk
    nt = (((1,), (1,)), ((), ()))

    def project(h):
        xb = x_ref[h * chunk:(h + 1) * chunk, :].astype(BF16)
        return (lax.dot_general(w_ref[:Q_DIM, :], xb, nt, preferred_element_type=F32),
                lax.dot_general(w_ref[Q_DIM:, :], xb, nt, preferred_element_type=F32))

    def rope(t, cos, sin):
        rot = t.reshape(t.shape[0] // HEAD_DIM, 2, half, chunk)
        first, second = rot[:, 0], rot[:, 1]
        return jnp.stack([first * cos - second * sin, second * cos + first * sin], axis=1).reshape(t.shape)

    nxt = project(0)
    for h in range(n_chunks):
        cols = slice(h * chunk, (h + 1) * chunk)
        q_t, kv_t = nxt[0] + b_ref[:Q_DIM, :], nxt[1] + b_ref[Q_DIM:, :]
        if h + 1 < n_chunks:
            nxt = project(h + 1)
        cos, sin = cos_ref[:, cols], sin_ref[:, cols]
        q_ref[:, cols] = (rope(q_t, cos, sin) * (HEAD_DIM ** -0.5 * LOG2E)).astype(BF16)
        k_ref[:, cols] = rope(kv_t[:KV_DIM], cos, sin)
        v_ref[:, cols] = kv_t[KV_DIM:]


def _qkv_rope(x, w_t, b_col, tabs, tm, tab_tiles):
    n = x.shape[0]
    tab_spec = pl.BlockSpec((HEAD_DIM // 2, tm), lambda i: (0, i % tab_tiles))
    col = lambda rows: pl.BlockSpec((rows, tm), lambda i: (0, i))
    return pl.pallas_call(
        _qkv_body,
        grid=(n // tm,),
        in_specs=[pl.BlockSpec((tm, D_MODEL), lambda i: (i, 0)),
                  _const_spec((QKV_DIM, D_MODEL)), _const_spec((QKV_DIM, 1)), tab_spec, tab_spec],
        out_specs=[col(Q_DIM), col(KV_DIM), col(KV_DIM)],
        out_shape=[jax.ShapeDtypeStruct((Q_DIM, n), BF16),
                   jax.ShapeDtypeStruct((KV_DIM, n), F32),
                   jax.ShapeDtypeStruct((KV_DIM, n), F32)],
        compiler_params=_params(1),
        name="qkv_rope",
    )(x, w_t, b_col, *tabs)


def _rope_tables(pos):
    half = HEAD_DIM // 2
    inv = ROPE_THETA ** (-np.arange(half, dtype=np.float64) / half)
    ang = inv[:, None] * np.asarray(pos, np.float64)[None, :]
    return tuple(jnp.asarray(t.astype(np.float32)) for t in (np.cos(ang), np.sin(ang)))


PAIR = 2 * CHUNK


def _attn_body(x_ref, q_ref, kp_ref, kc_ref, vp_ref, vc_ref, sink_ref, wot_ref, bo_ref, g_ref, b_ref,
               out_ref, ot_scr, *, tq, has_start):
    span = WINDOW + PAIR
    k_t = jnp.concatenate([kp_ref[...], kc_ref[...]], axis=1)
    v_t = jnp.concatenate([vp_ref[...], vc_ref[...]], axis=1).astype(BF16)
    k_rows = [k_t[m * HEAD_DIM:(m + 1) * HEAD_DIM].T.astype(BF16) for m in range(N_KV_HEADS)]
    ones_rows = jnp.ones((BF16_SUBLANES, v_t.shape[1]), BF16)
    v_aug = [jnp.concatenate([v_t[m * HEAD_DIM:(m + 1) * HEAD_DIM], ones_rows], axis=0) for m in range(N_KV_HEADS)]
    n_cols = GQA_GROUP * PAIR
    tok = lax.broadcasted_iota(jnp.int32, (CHUNK, n_cols), 1) & (PAIR - 1)
    first_chunk = tok < CHUNK
    if has_start:
        after_start = jnp.broadcast_to(pl.program_id(1) > 0, first_chunk.shape)

    def scores(pr, m):
        heads = range(GQA_GROUP * m, GQA_GROUP * (m + 1))
        q_m = jnp.concatenate([q_ref[h * HEAD_DIM:(h + 1) * HEAD_DIM, pr * PAIR:(pr + 1) * PAIR] for h in heads], axis=1)
        return jnp.dot(k_rows[m][pr * PAIR:pr * PAIR + span], q_m, preferred_element_type=F32)

    steps = [(pr, m) for pr in range(tq // PAIR) for m in range(N_KV_HEADS)]
    proj_pairs = min(2, tq // PAIR)
    queue = [scores(*st) for st in steps[:LOOKAHEAD]]
    for idx, (pr, m) in enumerate(steps):
        s = queue.pop(0)
        if idx + LOOKAHEAD < len(steps):
            queue.append(scores(*steps[idx + LOOKAHEAD]))
        rows = slice(pr * PAIR, (pr + 1) * PAIR)
        oldest, older = s[:CHUNK], s[CHUNK:2 * CHUNK]
        if has_start and pr == 0:
            oldest = jnp.where(after_start, oldest, NEG_INF)
            older = jnp.where(after_start, older, NEG_INF)
        edge = jnp.where(first_chunk, oldest, s[3 * CHUNK:])
        s = jnp.concatenate([edge, older, s[2 * CHUNK:3 * CHUNK]], axis=0)
        sink = sink_ref[m] * LOG2E
        mx = jnp.maximum(jnp.max(s, axis=0, keepdims=True), sink)
        e = jnp.exp2(s - mx)
        e_edge = e[:CHUNK]
        p_t = jnp.concatenate([jnp.where(first_chunk, e_edge, 0.0), e[CHUNK:],
                               jnp.where(first_chunk, 0.0, e_edge)], axis=0).astype(BF16)
        o_aug = jnp.dot(v_aug[m][:, pr * PAIR:pr * PAIR + span], p_t, preferred_element_type=F32)
        den = o_aug[HEAD_DIM:HEAD_DIM + 1] + jnp.exp2(sink - mx)
        o_t = o_aug[:HEAD_DIM] * (1.0 / den)
        for hb in range(GQA_GROUP):
            head = GQA_GROUP * m + hb
            ot_scr[head * HEAD_DIM:(head + 1) * HEAD_DIM, rows] = o_t[:, hb * PAIR:(hb + 1) * PAIR].astype(BF16)
        if m == N_KV_HEADS - 1 and (pr + 1) % proj_pairs == 0:
            blk = slice((pr + 1 - proj_pairs) * PAIR, (pr + 1) * PAIR)
            proj_t = jnp.dot(wot_ref[...], ot_scr[:, blk], preferred_element_type=F32)
            y = DN_ALPHA * x_ref[blk, :] + proj_t.T + bo_ref[...]
            out_ref[blk, :] = _layer_norm(y, g_ref[...], b_ref[...])


def _attn_proj_ln(x, q_t, k_prev, k_cur, v_prev, v_cur, sink_rows, wo_t, bo, gain, bias, *, n_batch, tq, prev_map, has_start):
    n = x.shape[0]
    tiles = n // (n_batch * tq)
    cur = lambda b, i: (b * tiles + i, 0)
    cur_t = lambda b, i: (0, b * tiles + i)
    return pl.pallas_call(
        functools.partial(_attn_body, tq=tq, has_start=has_start),
        grid=(n_batch, tiles),
        in_specs=[pl.BlockSpec((tq, D_MODEL), cur), pl.BlockSpec((Q_DIM, tq), cur_t),
                  pl.BlockSpec((KV_DIM, WINDOW), prev_map), pl.BlockSpec((KV_DIM, tq), cur_t),
                  pl.BlockSpec((KV_DIM, WINDOW), prev_map), pl.BlockSpec((KV_DIM, tq), cur_t),
                  _const_spec((N_KV_HEADS, 1, GQA_GROUP * PAIR)),
                  _const_spec((D_MODEL, Q_DIM)), _const_spec((1, D_MODEL)),
                  _const_spec((1, D_MODEL)), _const_spec((1, D_MODEL))],
        out_specs=pl.BlockSpec((tq, D_MODEL), cur),
        out_shape=jax.ShapeDtypeStruct((n, D_MODEL), F32),
        scratch_shapes=[pltpu.VMEM((Q_DIM, tq), BF16)],
        compiler_params=_params(2),
        name="attn_proj_ln",
    )(x, q_t, k_prev, k_cur, v_prev, v_cur, sink_rows, wo_t, bo, gain, bias)


def _ssm_prep_group(k, pcol_ref, prow_ref, bcol_ref, brow_ref, ccol_ref, dcol_ref, t_ref, wt_ref, v_ref, are_ref, aim_ref):
    def cmul(x_re, x_im, y_re, y_im):
        return x_re * y_re - x_im * y_im, x_re * y_im + x_im * y_re

    def a_bar(a_re, a_im, dt):
        mag = jnp.exp(dt * a_re)
        ang = dt * a_im
        return mag * jnp.cos(ang), mag * jnp.sin(ang)

    def zoh_factor(l_re, l_im, a_re, a_im):
        den = a_re * a_re + a_im * a_im
        n_re = l_re - 1.0
        return (n_re * a_re + l_im * a_im) / den, (l_im * a_re - n_re * a_im) / den

    n_bits = SSM_BLOCK.bit_length() - 1
    pc = pcol_ref[k]
    ar_c, ai_c, dt_c = pc[:, 0:1], pc[:, 1:2], jnp.exp(pc[:, 2:3])
    lam_c = a_bar(ar_c, ai_c, dt_c)
    fr_c, fi_c = zoh_factor(*lam_c, ar_c, ai_c)
    bc = bcol_ref[k]
    bbr_c = fr_c * bc[:, :SSM_GROUP] - fi_c * bc[:, SSM_GROUP:]
    bbi_c = fr_c * bc[:, SSM_GROUP:] + fi_c * bc[:, :SSM_GROUP]
    pr = prow_ref[k]
    ar_r, ai_r, dt_r = pr[0:1], pr[1:2], jnp.exp(pr[2:3])
    lam_r = a_bar(ar_r, ai_r, dt_r)
    fr_r, fi_r = zoh_factor(*lam_r, ar_r, ai_r)
    br = brow_ref[k]
    bbr_r = fr_r * br[:SSM_GROUP] - fi_r * br[SSM_GROUP:]
    bbi_r = fr_r * br[SSM_GROUP:] + fi_r * br[:SSM_GROUP]
    cc = ccol_ref[k]

    squares = [lam_c]
    for _ in range(n_bits - 1):
        squares.append(cmul(*squares[-1], *squares[-1]))

    def lane_pow(exps):
        acc = None
        for k, (q_re, q_im) in enumerate(squares):
            bit = ((exps >> k) & 1) == 1
            f_re, f_im = jnp.where(bit, q_re, 1.0), jnp.where(bit, q_im, 0.0)
            acc = (f_re, f_im) if acc is None else cmul(*acc, f_re, f_im)
        return acc

    lane = lax.broadcasted_iota(jnp.int32, (SSM_STATE, LANES), 1)
    lo = lane < SSM_BLOCK
    tau = lane & (SSM_BLOCK - 1)
    l0r, l0i = lane_pow(tau)
    l1r, l1i = cmul(l0r, l0i, *lam_c)
    lrr, lri = lane_pow(SSM_BLOCK - 1 - tau)

    g_re, g_im = [], []
    for j in range(SSM_GROUP // 2):
        sl = slice(j * LANES, (j + 1) * LANES)
        cr = jnp.where(lo, cc[:, 2 * j:2 * j + 1], cc[:, 2 * j + 1:2 * j + 2])
        ci = jnp.where(lo, cc[:, SSM_GROUP + 2 * j:SSM_GROUP + 2 * j + 1],
                       cc[:, SSM_GROUP + 2 * j + 1:SSM_GROUP + 2 * j + 2])
        g_re.append(cr * l0r - ci * l0i)
        g_im.append(-(cr * l0i + ci * l0r))
        v_ref[k, :SSM_STATE, sl] = (cr * l1r - ci * l1i).astype(BF16)
        v_ref[k, SSM_STATE:, sl] = (-(cr * l1i + ci * l1r)).astype(BF16)
        b_r = jnp.where(lo, bbr_c[:, 2 * j:2 * j + 1], bbr_c[:, 2 * j + 1:2 * j + 2])
        b_i = jnp.where(lo, bbi_c[:, 2 * j:2 * j + 1], bbi_c[:, 2 * j + 1:2 * j + 2])
        wt_ref[k, :SSM_STATE, sl] = (b_r * lrr - b_i * lri).astype(BF16)
        wt_ref[k, SSM_STATE:, sl] = (b_r * lri + b_i * lrr).astype(BF16)
    g_re = jnp.concatenate(g_re, axis=1)
    g_im = jnp.concatenate(g_im, axis=1)
    kvec = (jnp.dot(bbr_r, g_re, preferred_element_type=F32, precision=lax.Precision.HIGHEST)
            + jnp.dot(bbi_r, g_im, preferred_element_type=F32, precision=lax.Precision.HIGHEST))
    klane = lax.broadcasted_iota(jnp.int32, kvec.shape, 1)
    krow = lax.broadcasted_iota(jnp.int32, kvec.shape, 0)
    kvec = kvec + jnp.where(klane == krow * SSM_BLOCK, dcol_ref[k], 0.0)

    t_in = lax.broadcasted_iota(jnp.int32, (SSM_BLOCK, LANES), 0)
    causal = (lax.broadcasted_iota(jnp.int32, (SSM_BLOCK, LANES), 1) & (SSM_BLOCK - 1)) >= t_in
    for c in range(SSM_GROUP):
        for j in range(SSM_GROUP // 2):
            src = jnp.broadcast_to(kvec[c:c + 1, j * LANES:(j + 1) * LANES], (SSM_BLOCK, LANES))
            shifted = pltpu.roll(src, 0, 1, stride=1, stride_axis=0)
            t_ref[k, c * SSM_BLOCK:(c + 1) * SSM_BLOCK, j * LANES:(j + 1) * LANES] = (
                jnp.where(causal, shifted, 0.0).astype(BF16))
    a_re, a_im = lam_r
    for _ in range(n_bits):
        a_re, a_im = cmul(a_re, a_im, a_re, a_im)
    are_ref[k] = a_re
    aim_ref[k] = a_im


def _ssm_prep_inputs(log_dt, a_re, a_im, b_re, b_im, c_re, c_im, d):
    g, p, cg = SSM_GROUPS, SSM_STATE, SSM_GROUP
    params = jnp.stack([a_re, a_im, jnp.broadcast_to(log_dt[:, None], (g, p))], axis=-1)
    bcol = jnp.concatenate([b_re, b_im], axis=-1)
    ccol = jnp.concatenate([c_re.transpose(0, 2, 1), c_im.transpose(0, 2, 1)], axis=-1)
    return params, params.transpose(0, 2, 1), bcol, bcol.transpose(0, 2, 1), ccol, d.reshape(g, cg, 1)


def _ssm_in_body(x_ref, w_ref, b_ref, *rest, tm, groups):
    prep_refs, o_ref = rest[:-1 - 5 * bool(groups)], rest[-1 - 5 * bool(groups)]
    xb = x_ref[...].reshape(tm, D_MODEL).astype(BF16)
    ut = lax.dot_general(w_ref[...], xb, (((1,), (1,)), ((), ())), preferred_element_type=F32)
    ut = ut + b_ref[...]
    if len(o_ref.shape) == 2:
        o_ref[...] = ut
    else:
        for j in range(tm // LANES):
            o_ref[:, SUBLANES * j:SUBLANES * (j + 1), :] = (
                ut[:, j * LANES:(j + 1) * LANES].reshape(D_MODEL // SUBLANES, SUBLANES, LANES))
    for k in range(groups):
        _ssm_prep_group(k, *prep_refs, *rest[-5:])


def _ssm_in(x, x_spec, n_tiles, w_t, b_col, tm, prep_inputs=()):
    g, p, cg, n = SSM_GROUPS, SSM_STATE, SSM_GROUP, SSM_BLOCK_DIM
    groups = g // n_tiles if prep_inputs else 0
    spec = lambda *shape: pl.BlockSpec((groups,) + shape, lambda i: (i, 0, 0))
    prep_in = [spec(p, 3), spec(3, p), spec(p, 2 * cg), spec(2 * cg, p), spec(p, 2 * cg), spec(cg, 1)]
    prep_out = [spec(n, n), spec(2 * p, n), spec(2 * p, n), spec(1, p), spec(1, p)]
    prep_shapes = ([jax.ShapeDtypeStruct((g, n, n), BF16)] + [jax.ShapeDtypeStruct((g, 2 * p, n), BF16)] * 2
                   + [jax.ShapeDtypeStruct((g, 1, p), F32)] * 2)
    if groups:
        rows = SUBLANES * tm // LANES
        u_spec = pl.BlockSpec((D_MODEL // SUBLANES, rows, LANES), lambda i: (0, i, 0))
        u_shape = jax.ShapeDtypeStruct((D_MODEL // SUBLANES, n_tiles * rows, LANES), F32)
    else:
        u_spec = pl.BlockSpec((D_MODEL, tm), lambda i: (0, i))
        u_shape = jax.ShapeDtypeStruct((D_MODEL, n_tiles * tm), F32)
    return pl.pallas_call(
        functools.partial(_ssm_in_body, tm=tm, groups=groups),
        grid=(n_tiles,),
        in_specs=[x_spec, _const_spec((D_MODEL, D_MODEL)), _const_spec((D_MODEL, 1))] + (prep_in if groups else []),
        out_specs=[u_spec] + (prep_out if groups else []),
        out_shape=[u_shape] + (prep_shapes if groups else []),
        compiler_params=_params(1),
        name="ssm_in",
    )(x, w_t, b_col, *prep_inputs)


def _plane_rows(k, c, half_rows):
    octet = (SSM_GROUP // SUBLANES) * k + c // SUBLANES
    return pl.ds(octet * SUBLANES * half_rows + c % SUBLANES, half_rows, stride=SUBLANES)


def _ssm_core_group(k, x_ref, xs_ref, s0re_ref, s0im_ref, t_ref, wt_ref, v_ref, are_ref, aim_ref,
                    z_ref, zs_ref, sre_ref, sim_ref, pre_ref, pim_ref, s_in, *, half_rows, n_seq):
    n_prompt = 2 * half_rows
    n_rows = s_in.shape[1]
    ch0 = SSM_GROUP * k
    lo = lax.broadcasted_iota(jnp.int32, (half_rows, LANES), 1) < SSM_BLOCK
    lo_s = lax.broadcasted_iota(jnp.int32, (n_seq, LANES), 1) < SSM_BLOCK
    pad = jnp.zeros((n_rows - n_prompt - n_seq, LANES), F32)
    cols = []
    for j in range(SSM_GROUP // 2):
        xa = x_ref[_plane_rows(k, 2 * j, half_rows), :]
        xb = x_ref[_plane_rows(k, 2 * j + 1, half_rows), :]
        first = jnp.where(lo, xa, pltpu.roll(xb, SSM_BLOCK, 1))
        second = jnp.where(lo, pltpu.roll(xa, SSM_BLOCK, 1), xb)
        samp = jnp.where(lo_s, xs_ref[ch0 + 2 * j], pltpu.roll(xs_ref[ch0 + 2 * j + 1], SSM_BLOCK, 1))
        cols.append(jnp.concatenate([first, second, samp, pad], axis=0).astype(BF16))
    u = jnp.concatenate(cols, axis=1)

    inj = lax.dot_general(u, wt_ref[k], (((1,), (1,)), ((), ())), preferred_element_type=F32)
    inj_re, inj_im = inj[:, :SSM_STATE], inj[:, SSM_STATE:]
    y_u = jnp.dot(u, t_ref[k], preferred_element_type=F32)
    a_re, a_im = are_ref[k], aim_ref[k]

    s_re = jnp.zeros((n_seq, SSM_STATE), F32)
    s_im = jnp.zeros((n_seq, SSM_STATE), F32)
    for n in range(n_prompt // n_seq):
        r = slice(n * n_seq, (n + 1) * n_seq)
        s_in[k, r, :SSM_STATE] = s_re
        s_in[k, r, SSM_STATE:] = s_im
        s_re, s_im = (a_re * s_re - a_im * s_im + inj_re[r], a_re * s_im + a_im * s_re + inj_im[r])
    pre_ref[k] = s_re
    pim_ref[k] = s_im
    rs = slice(n_prompt, n_prompt + n_seq)
    s0_re, s0_im = s0re_ref[k], s0im_ref[k]
    s_in[k, rs, :SSM_STATE] = s0_re
    s_in[k, rs, SSM_STATE:] = s0_im
    s_in[k, n_prompt + n_seq:, :] = jnp.zeros((n_rows - n_prompt - n_seq, 2 * SSM_STATE), F32)
    sre_ref[k] = a_re * s0_re - a_im * s0_im + inj_re[rs]
    sim_ref[k] = a_re * s0_im + a_im * s0_re + inj_im[rs]

    s_bf = s_in[k].astype(BF16)
    width = 2 * LANES
    for h in range(SSM_BLOCK_DIM // width):
        y = y_u[:, h * width:(h + 1) * width] + jnp.dot(s_bf, v_ref[k, :, h * width:(h + 1) * width],
                                                        preferred_element_type=F32)
        z = _gelu_tanh(y)
        for jj in range(width // LANES):
            j = h * (width // LANES) + jj
            blk = z[:, jj * LANES:(jj + 1) * LANES]
            first, second = blk[:half_rows], blk[half_rows:n_prompt]
            z_ref[_plane_rows(k, 2 * j, half_rows), :] = jnp.where(lo, first, pltpu.roll(second, SSM_BLOCK, 1))
            z_ref[_plane_rows(k, 2 * j + 1, half_rows), :] = jnp.where(lo, pltpu.roll(first, SSM_BLOCK, 1), second)
            samp = blk[rs]
            zs_ref[ch0 + 2 * j] = samp
            zs_ref[ch0 + 2 * j + 1] = pltpu.roll(samp, SSM_BLOCK, 1)


def _ssm_core_body(*refs, half_rows, n_seq, groups):
    for k in range(groups):
        _ssm_core_group(k, *refs, half_rows=half_rows, n_seq=n_seq)


def _ssm_core(ut, uts3, s0_re, s0_im, ops, *, half_rows, n_seq):
    t_op, wt, v_op, a_re, a_im = ops
    g, p, cg, n = SSM_GROUPS, SSM_STATE, SSM_GROUP, SSM_BLOCK_DIM
    n_rows = 2 * half_rows + 2 * n_seq
    groups = SSM_GROUPS_PER_STEP
    grp = lambda *shape: pl.BlockSpec((groups,) + shape, lambda i: (i, 0, 0))
    chan = lambda r: pl.BlockSpec((groups * cg, r, LANES), lambda i: (i, 0, 0))
    rows = pl.BlockSpec((groups * cg * half_rows, LANES), lambda i: (i, 0))
    state = jax.ShapeDtypeStruct((g, n_seq, p), F32)
    return pl.pallas_call(
        functools.partial(_ssm_core_body, half_rows=half_rows, n_seq=n_seq, groups=groups),
        grid=(g // groups,),
        in_specs=[rows, chan(n_seq), grp(n_seq, p), grp(n_seq, p),
                  grp(n, n), grp(2 * p, n), grp(2 * p, n), grp(1, p), grp(1, p)],
        out_specs=[rows, chan(n_seq), grp(n_seq, p), grp(n_seq, p), grp(n_seq, p), grp(n_seq, p)],
        out_shape=[jax.ShapeDtypeStruct(ut.shape, F32), jax.ShapeDtypeStruct(uts3.shape, F32),
                   state, state, state, state],
        scratch_shapes=[pltpu.VMEM((groups, n_rows, 2 * p), F32)],
        compiler_params=_params(1),
        name="ssm_core",
    )(ut, uts3, s0_re, s0_im, t_op, wt, v_op, a_re, a_im)


def _glu_body(x_ref, zt_ref, w_ref, b_ref, g_ref, bb_ref, o_ref, *, tm):
    chunk = 2 * LANES
    lead = chunk // (tm // x_ref.shape[0])

    def gate_values(h):
        if len(zt_ref.shape) == 2:
            zt = zt_ref[:, h * chunk:(h + 1) * chunk]
        else:
            zt = jnp.concatenate(
                [zt_ref[:, SUBLANES * j:SUBLANES * (j + 1), :].reshape(D_MODEL, LANES)
                 for j in range(h * chunk // LANES, (h + 1) * chunk // LANES)], axis=1).astype(BF16)
        return (jnp.dot(w_ref[:D_MODEL, :], zt, preferred_element_type=F32),
                jnp.dot(w_ref[D_MODEL:, :], zt, preferred_element_type=F32))

    queue = [gate_values(h) for h in range(min(LOOKAHEAD, tm // chunk))]
    for h in range(tm // chunk):
        gv = queue.pop(0)
        value, gate = gv[0] + b_ref[:D_MODEL, :], gv[1] + b_ref[D_MODEL:, :]
        if h + LOOKAHEAD < tm // chunk:
            queue.append(gate_values(h + LOOKAHEAD))
        mixed = value * (jnp.tanh(gate) + 1.0)
        x = x_ref[h * lead:(h + 1) * lead].reshape(chunk, D_MODEL)
        y = DN_ALPHA * x + mixed.T
        o_ref[h * lead:(h + 1) * lead] = _layer_norm(y, g_ref[...], bb_ref[...]).reshape((lead,) + o_ref.shape[1:])


def _glu_ln(x, x_spec, n_tiles, zt, w_t, b_col, gain, bias, tm):
    if zt.ndim == 3:
        z_spec = pl.BlockSpec((D_MODEL // SUBLANES, SUBLANES * tm // LANES, LANES), lambda i: (0, i, 0))
    else:
        z_spec = pl.BlockSpec((D_MODEL, tm), lambda i: (0, i))
    return pl.pallas_call(
        functools.partial(_glu_body, tm=tm),
        grid=(n_tiles,),
        in_specs=[x_spec, z_spec,
                  _const_spec((2 * D_MODEL, D_MODEL)), _const_spec((2 * D_MODEL, 1)),
                  _const_spec((1, D_MODEL)), _const_spec((1, D_MODEL))],
        out_specs=x_spec,
        out_shape=jax.ShapeDtypeStruct(x.shape, F32),
        compiler_params=_params(1),
        name="glu_ln",
    )(x, zt, w_t, b_col, gain, bias)


def kernel(x_prompt, x_sample, cache_k, cache_v, state_ssm_re, state_ssm_im, attn_w_qkv, attn_b_qkv, attn_sinks, attn_w_o, attn_b_o, ssm_w_in, ssm_b_in, ssm_log_dt, ssm_a_re, ssm_a_im, ssm_b_re, ssm_b_im, ssm_c_re, ssm_c_im, ssm_d, ssm_w_glu, ssm_b_glu, ffn_w_up, ffn_w_down, ln_gain, ln_bias):
    n_b, seq, _ = x_prompt.shape
    n_s, seq_s, _ = x_sample.shape
    n_p, n_d = n_b * seq, n_s * seq_s
    blocks = seq // SSM_BLOCK
    half_rows = (blocks // 2) * n_b
    ssm_tm = 2 * SSM_BLOCK * n_b

    xp = x_prompt.reshape(n_p, D_MODEL)
    xs = x_sample.reshape(n_d, D_MODEL)
    tabs_p = _rope_tables(np.arange(seq))
    tabs_s = _rope_tables(np.tile(PAST_LEN + np.arange(seq_s), n_s))

    ssm_view = (n_b, 2, blocks // 2, SSM_BLOCK, D_MODEL)
    ssm_spec = pl.BlockSpec((n_b, 2, 1, SSM_BLOCK, D_MODEL), lambda i: (0, 0, i, 0, 0))
    flat_spec = pl.BlockSpec((n_d, D_MODEL), lambda i: (i, 0))

    w_up_all = ffn_w_up.astype(BF16)
    w_dn_all = ffn_w_down.astype(BF16)
    new_k_p, new_v_p, new_re_p, new_im_p = [], [], [], []
    new_k_s, new_v_s, new_re_s, new_im_s = [], [], [], []
    for i in range(DEPTH):
        l = i // 2
        gain = ln_gain[i][:, None, :]
        bias = ln_bias[i][:, None, :]
        if i % 2 == 0:
            w_qkv_t = attn_w_qkv[l].T.astype(BF16)
            b_qkv = attn_b_qkv[l][:, None]
            w_o_t = attn_w_o[l].T.astype(BF16)
            b_o = attn_b_o[l][None]
            sink_rows = jnp.repeat(attn_sinks[l].astype(F32), PAIR).reshape(N_KV_HEADS, 1, GQA_GROUP * PAIR)
            qp, kp, vp = _qkv_rope(xp, w_qkv_t, b_qkv, tabs_p, QKV_TILE, seq // QKV_TILE)
            qs, ks, vs = _qkv_rope(xs, w_qkv_t, b_qkv, tabs_s, n_d, 1)
            prev_p = lambda b, t: (0, jnp.maximum(b * (seq // WINDOW) + t * (ATTN_TILE // WINDOW) - 1, 0))
            xp = _attn_proj_ln(xp, qp, kp, kp, vp, vp, sink_rows, w_o_t, b_o, gain[0], bias[0],
                               n_batch=n_b, tq=ATTN_TILE, prev_map=prev_p, has_start=True)
            pad_rows = lambda a: jnp.pad(a.reshape(n_s, seq_s, -1), ((0, 0), (0, PAIR - seq_s), (0, 0))).reshape(n_s * PAIR, -1)
            pad_cols = lambda a: jnp.pad(a.reshape(-1, n_s, seq_s), ((0, 0), (0, 0), (0, PAIR - seq_s))).reshape(-1, n_s * PAIR)
            ck_t = cache_k[l].reshape(n_s * WINDOW, KV_DIM).T
            cv_t = cache_v[l].reshape(n_s * WINDOW, KV_DIM).T
            xs = _attn_proj_ln(pad_rows(xs), pad_cols(qs), ck_t, pad_cols(ks), cv_t, pad_cols(vs), sink_rows, w_o_t, b_o,
                               gain[0], bias[0], n_batch=n_s, tq=PAIR, prev_map=lambda b, t: (0, b), has_start=False)
            xs = xs.reshape(n_s, PAIR, D_MODEL)[:, :seq_s].reshape(n_d, D_MODEL)
            last = lambda a: (jnp.stack([a[:, (b + 1) * seq - WINDOW:(b + 1) * seq] for b in range(n_b)])
                              .transpose(0, 2, 1).reshape(n_b, WINDOW, N_KV_HEADS, HEAD_DIM))
            new_k_p.append(last(kp))
            new_v_p.append(last(vp))
            ks4 = ks.T.reshape(n_s, seq_s, N_KV_HEADS, HEAD_DIM)
            vs4 = vs.T.reshape(n_s, seq_s, N_KV_HEADS, HEAD_DIM)
            new_k_s.append(jnp.concatenate([cache_k[l], ks4], axis=1)[:, -WINDOW:])
            new_v_s.append(jnp.concatenate([cache_v[l], vs4], axis=1)[:, -WINDOW:])
        else:
            w_in_t = ssm_w_in[l].T.astype(BF16)
            b_in = ssm_b_in[l][:, None]
            w_glu_t = (0.5 * ssm_w_glu[l]).T.astype(BF16)
            b_glu = 0.5 * ssm_b_glu[l][:, None]
            prep_inputs = _ssm_prep_inputs(ssm_log_dt[l], ssm_a_re[l], ssm_a_im[l], ssm_b_re[l], ssm_b_im[l],
                                           ssm_c_re[l], ssm_c_im[l], ssm_d[l])
            ut, *ops = _ssm_in(xp.reshape(ssm_view), ssm_spec, blocks // 2, w_in_t, b_in, ssm_tm, prep_inputs)
            uts, = _ssm_in(xs, flat_spec, 1, w_in_t, b_in, n_d)
            uts3 = jnp.pad(uts.reshape(D_MODEL, n_s, seq_s), ((0, 0), (0, 0), (0, LANES - seq_s)))
            s0_re = state_ssm_re[l].transpose(1, 0, 2)
            s0_im = state_ssm_im[l].transpose(1, 0, 2)
            zt, zs3, s_re, s_im, p_re, p_im = _ssm_core(ut.reshape(-1, LANES), uts3, s0_re, s0_im, ops,
                                                        half_rows=half_rows, n_seq=n_s)
            zst = zs3[:, :, :seq_s].reshape(D_MODEL, n_d).astype(BF16)
            xp = _glu_ln(xp.reshape(ssm_view), ssm_spec, blocks // 2, zt.reshape(ut.shape), w_glu_t, b_glu,
                         gain[0], bias[0], ssm_tm).reshape(n_p, D_MODEL)
            xs = _glu_ln(xs, flat_spec, 1, zst, w_glu_t, b_glu, gain[0], bias[0], n_d)
            new_re_p.append(p_re.transpose(1, 0, 2))
            new_im_p.append(p_im.transpose(1, 0, 2))
            new_re_s.append(s_re.transpose(1, 0, 2))
            new_im_s.append(s_im.transpose(1, 0, 2))
        xp = _ffn_ln(xp, w_up_all, w_dn_all, i, gain[1], bias[1], FFN_TILE)
        xs = _ffn_ln(xs, w_up_all, w_dn_all, i, gain[1], bias[1], n_d)
    return (xp.reshape(x_prompt.shape), xs.reshape(x_sample.shape),
            jnp.stack(new_k_p), jnp.stack(new_v_p), jnp.stack(new_re_p), jnp.stack(new_im_p),
            jnp.stack(new_k_s), jnp.stack(new_v_s), jnp.stack(new_re_s), jnp.stack(new_im_s))
```
